```python
import math
import jax, jax.numpy as jnp
from jax import lax
import numpy as np

D_MODEL = 1024
BATCH = 32
SEQ = 2048
DEPTH = 4
DEC_BATCH = 16
DEC_SEQ = 32
PAST_LEN = 1024

CHUNK = 64
Q_BLOCK = 128
N_MIXERS = 2
N_SSD = (DEPTH + 1) // 2
N_SB = DEPTH // 2
SSD_EXPAND = 2
D_INNER = SSD_EXPAND * D_MODEL
SSD_HEADDIM = 64
SSD_HEADS = D_INNER // SSD_HEADDIM
SSD_GROUPS = 8
SSD_HPG = SSD_HEADS // SSD_GROUPS
SSD_STATE = 128
SSD_CONV = 4
CONV_DIM = D_INNER + 2 * SSD_GROUPS * SSD_STATE
D_IN_PROJ = 2 * D_INNER + 2 * SSD_GROUPS * SSD_STATE + SSD_HEADS
SB_HEADS = 16
SB_HEADDIM = D_MODEL // SB_HEADS
SB_SCALE = 1.0 / math.sqrt(SB_HEADDIM)
D_FF = 4 * D_MODEL
EPS = 1e-6

kernel_name = 'hybrid_ssd_stickbreaking_stream_step'


def rms_norm(x, g):
    x32 = x.astype(jnp.float32)
    y = x32 * lax.rsqrt(jnp.mean(x32 * x32, axis=-1, keepdims=True) + EPS)
    return (y * g.astype(jnp.float32)).astype(x.dtype)


def sq_relu_mlp(h, w_up, w_down):
    return jnp.square(jax.nn.relu(h @ w_up)) @ w_down


def causal_conv(xbc, conv_state, w, b):
    seqlen = xbc.shape[1]
    xpad = jnp.concatenate([conv_state.astype(xbc.dtype), xbc], axis=1)
    out = b
    for tap in range(SSD_CONV):
        out = out + w[tap] * xpad[:, tap:tap + seqlen]
    return jax.nn.silu(out), xpad[:, -(SSD_CONV - 1):]


def ssd_chunk_scan(x, dt, a, bm, cm, h0):
    bsz, seqlen = x.shape[0], x.shape[1]
    clen = min(CHUNK, seqlen)
    nc = seqlen // clen
    f32 = jnp.float32

    def to_chunks(t):
        return jnp.moveaxis(t.reshape((bsz, nc, clen) + t.shape[2:]), 1, 0)

    xdt = (x.astype(f32) * dt[..., None]).reshape(bsz, seqlen, SSD_GROUPS, SSD_HPG, SSD_HEADDIM)
    a_dt = (dt * a).reshape(bsz, seqlen, SSD_GROUPS, SSD_HPG)
    causal = jnp.tril(jnp.ones((clen, clen), dtype=bool))

    def step(h, inp):
        xdt_c, adt_c, b_c, c_c = inp
        a_cs = jnp.cumsum(adt_c, axis=1)
        seg = a_cs[:, :, None] - a_cs[:, None, :]
        decay = jnp.exp(jnp.where(causal[None, :, :, None, None], seg, -jnp.inf))
        cb = jnp.einsum('btgn,bsgn->btsg', c_c, b_c)
        y_diag = jnp.einsum('btsg,btsgr,bsgrp->btgrp', cb, decay, xdt_c)
        y_off = jnp.einsum('btgn,bgrpn,btgr->btgrp', c_c, h, jnp.exp(a_cs))
        decay_end = jnp.exp(a_cs[:, -1:] - a_cs)
        h_new = h * jnp.exp(a_cs[:, -1])[..., None, None] + jnp.einsum(
            'bsgn,bsgr,bsgrp->bgrpn', b_c, decay_end, xdt_c)
        return h_new, y_diag + y_off

    h_init = h0.astype(f32).reshape(bsz, SSD_GROUPS, SSD_HPG, SSD_HEADDIM, SSD_STATE)
    h_fin, y = lax.scan(step, h_init, (to_chunks(xdt), to_chunks(a_dt),
                                       to_chunks(bm.astype(f32)), to_chunks(cm.astype(f32))))
    y = jnp.moveaxis(y, 0, 1).reshape(bsz, seqlen, SSD_HEADS, SSD_HEADDIM)
    return y, h_fin.reshape(bsz, SSD_HEADS, SSD_HEADDIM, SSD_STATE)


def ssd_mixer(u, conv_state, ssm_state, in_w, conv_w, conv_b, dt_bias, a_log, d_skip, norm_w, out_w):
    bsz, seqlen, _ = u.shape
    f32 = jnp.float32
    zxbcdt = u @ in_w
    z, xbc, dt_raw = jnp.split(zxbcdt, [D_INNER, D_INNER + CONV_DIM], axis=-1)
    xbc, new_conv = causal_conv(xbc, conv_state, conv_w, conv_b)
    xs, bm, cm = jnp.split(xbc, [D_INNER, D_INNER + SSD_GROUPS * SSD_STATE], axis=-1)
    dt = jax.nn.softplus((dt_raw + dt_bias).astype(f32))
    a = -jnp.exp(a_log.astype(f32))
    x_h = xs.reshape(bsz, seqlen, SSD_HEADS, SSD_HEADDIM)
    y, h_new = ssd_chunk_scan(x_h, dt, a,
                              bm.reshape(bsz, seqlen, SSD_GROUPS, SSD_STATE),
                              cm.reshape(bsz, seqlen, SSD_GROUPS, SSD_STATE), ssm_state)
    y = y + d_skip.astype(f32)[:, None] * x_h.astype(f32)
    y = y.reshape(bsz, seqlen, D_INNER) * jax.nn.silu(z.astype(f32))
    yg = y.reshape(bsz, seqlen, SSD_GROUPS, D_INNER // SSD_GROUPS)
    yg = yg * lax.rsqrt(jnp.mean(yg * yg, axis=-1, keepdims=True) + EPS)
    y = (yg.reshape(bsz, seqlen, D_INNER) * norm_w.astype(f32)).astype(u.dtype)
    return y @ out_w, new_conv, h_new.astype(u.dtype)


def sb_project(u, qkv_w, q_gain, k_gain):
    q, k, v = jnp.split(u @ qkv_w, 3, axis=-1)
    shape = u.shape[:2] + (SB_HEADS, SB_HEADDIM)
    return rms_norm(q.reshape(shape), q_gain), rms_norm(k.reshape(shape), k_gain), v.reshape(shape)


def sb_attend(q, k, v, q_pos, k_pos):
    z = jnp.einsum('bqhd,bkhd->bhqk', q, k).astype(jnp.float32) * SB_SCALE
    mask = k_pos[None, :] < q_pos[:, None]
    log_1mb = jnp.where(mask, jax.nn.log_sigmoid(-z), 0.0)
    rev = lax.cumsum(log_1mb, axis=3, reverse=True)
    after = jnp.concatenate([rev[..., 1:], jnp.zeros_like(rev[..., :1])], axis=-1)
    w = jnp.where(mask, jnp.exp(jax.nn.log_sigmoid(z) + after), 0.0)
    return jnp.einsum('bhqk,bkhd->bqhd', w, v.astype(jnp.float32)).astype(v.dtype)


def sb_prompt(q, k, v):
    outs = []
    for blk in range(q.shape[1] // Q_BLOCK):
        start, end = blk * Q_BLOCK, (blk + 1) * Q_BLOCK
        outs.append(sb_attend(q[:, start:end], k[:, :end], v[:, :end],
                              jnp.arange(start, end), jnp.arange(end)))
    return jnp.concatenate(outs, axis=1)


def setup_inputs(seed: int = 0) -> dict:
    key = jax.random.key(seed)
    ks = jax.random.split(key, 24)
    f32 = jnp.float32

    def nrm(k, shape, scale):
        return scale * jax.random.normal(k, shape, f32)

    dt_init = jnp.exp(jax.random.uniform(ks[10], (N_SSD, SSD_HEADS), f32,
                                         math.log(1e-3), math.log(1e-1)))
    dt_bias = dt_init + jnp.log(-jnp.expm1(-dt_init))
    a_log = jnp.log(jax.random.uniform(ks[11], (N_SSD, SSD_HEADS), f32, 1.0, 16.0))
    return {
        'x_prompt': nrm(ks[0], (BATCH, SEQ, D_MODEL), 1.0),
        'x_sample': nrm(ks[1], (DEC_BATCH, DEC_SEQ, D_MODEL), 1.0),
        'state_ssm': nrm(ks[2], (N_SSD, DEC_BATCH, SSD_HEADS, SSD_HEADDIM, SSD_STATE), 0.1),
        'state_conv': nrm(ks[3], (N_SSD, DEC_BATCH, SSD_CONV - 1, CONV_DIM), 1.0),
        'cache_k': nrm(ks[4], (N_SB, DEC_BATCH, PAST_LEN, SB_HEADS, SB_HEADDIM), 1.0),
        'cache_v': nrm(ks[5], (N_SB, DEC_BATCH, PAST_LEN, SB_HEADS, SB_HEADDIM), 1.0),
        'norm_mix': 1.0 + nrm(ks[6], (DEPTH, D_MODEL), 0.01),
        'norm_mlp': 1.0 + nrm(ks[7], (DEPTH, D_MODEL), 0.01),
        'ssd_in_w': nrm(ks[8], (N_SSD, D_MODEL, D_IN_PROJ), D_MODEL ** -0.5),
        'ssd_conv_w': nrm(ks[9], (N_SSD, SSD_CONV, CONV_DIM), SSD_CONV ** -0.5),
        'ssd_conv_b': nrm(ks[12], (N_SSD, CONV_DIM), 0.01),
        'ssd_dt_bias': dt_bias,
        'ssd_a_log': a_log,
        'ssd_d': 1.0 + nrm(ks[13], (N_SSD, SSD_HEADS), 0.01),
        'ssd_norm_w': 1.0 + nrm(ks[14], (N_SSD, D_INNER), 0.01),
        'ssd_out_w': nrm(ks[15], (N_SSD, D_INNER, D_MODEL), D_INNER ** -0.5),
        'sb_qkv_w': nrm(ks[16], (N_SB, D_MODEL, 3 * D_MODEL), D_MODEL ** -0.5),
        'sb_q_gain': 1.0 + nrm(ks[17], (N_SB, SB_HEADDIM), 0.01),
        'sb_k_gain': 1.0 + nrm(ks[18], (N_SB, SB_HEADDIM), 0.01),
        'sb_out_w': nrm(ks[19], (N_SB, D_MODEL, D_MODEL), D_MODEL ** -0.5),
        'mlp_up': nrm(ks[20], (DEPTH, D_MODEL, D_FF), D_MODEL ** -0.5),
        'mlp_down': nrm(ks[21], (DEPTH, D_FF, D_MODEL), 0.5 * D_FF ** -0.5),
    }


def reference(x_prompt, x_sample, state_ssm, state_conv, cache_k, cache_v,
              norm_mix, norm_mlp, ssd_in_w, ssd_conv_w, ssd_conv_b, ssd_dt_bias, ssd_a_log,
              ssd_d, ssd_norm_w, ssd_out_w, sb_qkv_w, sb_q_gain, sb_k_gain, sb_out_w,
              mlp_up, mlp_down):
    bsz_p, seq_p = x_prompt.shape[0], x_prompt.shape[1]
    bsz_s, seq_s = x_sample.shape[0], x_sample.shape[1]
    past = cache_k.shape[2]
    y_p, y_s = x_prompt, x_sample
    p_ssm, p_conv, p_k, p_v = [], [], [], []
    s_ssm, s_conv, s_k, s_v = [], [], [], []
    for i in range(DEPTH):
        j = i // N_MIXERS
        h_p = rms_norm(y_p, norm_mix[i])
        h_s = rms_norm(y_s, norm_mix[i])
        if i % N_MIXERS == 0:
            params = (ssd_in_w[j], ssd_conv_w[j], ssd_conv_b[j], ssd_dt_bias[j], ssd_a_log[j],
                      ssd_d[j], ssd_norm_w[j], ssd_out_w[j])
            conv0 = jnp.zeros((bsz_p, SSD_CONV - 1, CONV_DIM), x_prompt.dtype)
            ssm0 = jnp.zeros((bsz_p, SSD_HEADS, SSD_HEADDIM, SSD_STATE), jnp.float32)
            m_p, c_p, st_p = ssd_mixer(h_p, conv0, ssm0, *params)
            m_s, c_s, st_s = ssd_mixer(h_s, state_conv[j], state_ssm[j], *params)
            p_ssm.append(st_p); p_conv.append(c_p)
            s_ssm.append(st_s); s_conv.append(c_s)
        else:
            q_p, k_p, v_p = sb_project(h_p, sb_qkv_w[j], sb_q_gain[j], sb_k_gain[j])
            o_p = sb_prompt(q_p, k_p, v_p)
            m_p = o_p.reshape(bsz_p, seq_p, D_MODEL) @ sb_out_w[j]
            q_s, k_s, v_s = sb_project(h_s, sb_qkv_w[j], sb_q_gain[j], sb_k_gain[j])
            k_all = jnp.concatenate([cache_k[j].astype(k_s.dtype), k_s], axis=1)
            v_all = jnp.concatenate([cache_v[j].astype(v_s.dtype), v_s], axis=1)
            o_s = sb_attend(q_s, k_all, v_all, past + jnp.arange(seq_s), jnp.arange(past + seq_s))
            m_s = o_s.reshape(bsz_s, seq_s, D_MODEL) @ sb_out_w[j]
            p_k.append(k_p); p_v.append(v_p)
            s_k.append(k_s); s_v.append(v_s)
        y_p = y_p + m_p
        y_s = y_s + m_s
        y_p = y_p + sq_relu_mlp(rms_norm(y_p, norm_mlp[i]), mlp_up[i], mlp_down[i])
        y_s = y_s + sq_relu_mlp(rms_norm(y_s, norm_mlp[i]), mlp_up[i], mlp_down[i])
    prompt_state_ssm = jnp.stack(p_ssm)
    prompt_state_conv = jnp.stack(p_conv)
    prompt_k = jnp.stack(p_k)
    prompt_v = jnp.stack(p_v)
    sample_state_ssm = jnp.stack(s_ssm)
    sample_state_conv = jnp.stack(s_conv)
    sample_k = jnp.stack(s_k)
    sample_v = jnp.stack(s_v)
    return (y_p, y_s, prompt_state_ssm, prompt_state_conv, prompt_k, prompt_v,
            sample_state_ssm, sample_state_conv, sample_k, sample_v)
```

```python
import functools

import jax
import jax.numpy as jnp
from jax import lax
from jax.experimental import pallas as pl
from jax.experimental.pallas import tpu as pltpu

F32 = jnp.float32
BF16 = jnp.bfloat16
EPS = 1e-6

LANES = 128
SUBLANES = 8
VMEM_LIMIT_BYTES = 56 * 1024 * 1024

SSD_HEADDIM = 64
SSD_GROUPS = 8
SSD_STATE = 128
SSD_CONV = 4
SB_HEADDIM = 64
SSD_CHUNK = 128
KEY_BLOCK = 128
VREGS_PER_KEY_BLOCK = KEY_BLOCK // SUBLANES


def _params(*sem):
    return pltpu.CompilerParams(dimension_semantics=sem, vmem_limit_bytes=VMEM_LIMIT_BYTES)


def _rms_scale(x, g):
    ms = jnp.mean(x * x, axis=-1, keepdims=True)
    return x * lax.rsqrt(ms + EPS) * g


def _sigmoid(x):
    return 1.0 / (1.0 + jnp.exp(-x))


def _softplus(x):
    return jnp.maximum(x, 0.0) + jnp.log(1.0 + jnp.exp(-jnp.abs(x)))


def _norm_matmul_body(x_ref, g_ref, w_ref, o_ref, xn_ref):
    @pl.when(pl.program_id(1) == 0)
    def _():
        xn_ref[...] = _rms_scale(x_ref[...], g_ref[...]).astype(BF16)

    o_ref[...] = jnp.dot(xn_ref[...], w_ref[...], preferred_element_type=F32)


def norm_matmul(x, g, w, tm, tn):
    t, d = x.shape
    n = w.shape[1]
    return pl.pallas_call(
        _norm_matmul_body,
        grid=(t // tm, n // tn),
        in_specs=[pl.BlockSpec((tm, d), lambda i, j: (i, 0)),
                  pl.BlockSpec((1, d), lambda i, j: (0, 0)),
                  pl.BlockSpec((d, tn), lambda i, j: (0, j))],
        out_specs=pl.BlockSpec((tm, tn), lambda i, j: (i, j)),
        out_shape=jax.ShapeDtypeStruct((t, n), F32),
        scratch_shapes=[pltpu.VMEM((tm, d), BF16)],
        compiler_params=_params("arbitrary", "arbitrary"),
        name="norm_matmul",
    )(x, g.reshape(1, d), w)


def _matmul_res_body(a_ref, w_ref, r_ref, o_ref):
    o_ref[...] = r_ref[...] + jnp.dot(a_ref[...].astype(BF16), w_ref[...],
                                      preferred_element_type=F32)


def matmul_res(a, w, res, tm):
    t, k = a.shape
    n = w.shape[1]
    return pl.pallas_call(
        _matmul_res_body,
        grid=(t // tm,),
        in_specs=[pl.BlockSpec((tm, k), lambda i: (i, 0)),
                  pl.BlockSpec((k, n), lambda i: (0, 0)),
                  pl.BlockSpec((tm, n), lambda i: (i, 0))],
        out_specs=pl.BlockSpec((tm, n), lambda i: (i, 0)),
        out_shape=jax.ShapeDtypeStruct((t, n), F32),
        compiler_params=_params("arbitrary"),
        name="matmul_res",
    )(a, w, res)


def _mlp_body(x_ref, g_ref, wu_ref, wd_ref, o_ref, xn_ref):
    @pl.when(pl.program_id(1) == 0)
    def _():
        x = x_ref[...]
        xn_ref[...] = _rms_scale(x, g_ref[...]).astype(BF16)
        o_ref[...] = x

    h = jnp.dot(xn_ref[...], wu_ref[...], preferred_element_type=F32)
    h = jnp.maximum(h, 0.0)
    h = (h * h).astype(BF16)
    o_ref[...] += jnp.dot(h, wd_ref[...], preferred_element_type=F32)


def mlp(x, g, wu, wd, tm, tf):
    t, d = x.shape
    f = wu.shape[1]
    return pl.pallas_call(
        _mlp_body,
        grid=(t // tm, f // tf),
        in_specs=[pl.BlockSpec((tm, d), lambda i, j: (i, 0)),
                  pl.BlockSpec((1, d), lambda i, j: (0, 0)),
                  pl.BlockSpec((d, tf), lambda i, j: (0, j)),
                  pl.BlockSpec((tf, d), lambda i, j: (j, 0))],
        out_specs=pl.BlockSpec((tm, d), lambda i, j: (i, 0)),
        out_shape=jax.ShapeDtypeStruct((t, d), F32),
        scratch_shapes=[pltpu.VMEM((tm, d), BF16)],
        compiler_params=_params("arbitrary", "arbitrary"),
        name="mlp",
    )(x, g.reshape(1, d), wu, wd)


def _ssd_body(*refs, valid, zero_init, n_chunks, n_heads):
    if zero_init:
        (z_ref, x_ref, b_ref, c_ref, dt_ref, cw_ref, cb_ref, dtb_ref, alog_ref, dsk_ref, nw_ref,
         y_ref, cso_ref, sso_ref, xpad_ref, act_ref, ht_ref) = refs
        cs_ref = ss_ref = None
    else:
        (z_ref, x_ref, b_ref, c_ref, dt_ref, cw_ref, cb_ref, dtb_ref, alog_ref, dsk_ref, nw_ref,
         cs_ref, ss_ref, y_ref, cso_ref, sso_ref, xpad_ref, act_ref, ht_ref) = refs
    L = SSD_CHUNK
    G = SSD_GROUPS
    R = n_heads // G
    P = SSD_HEADDIM
    GP = R * P
    d_inner = n_heads * P
    gn = G * SSD_STATE
    conv_dim = d_inner + 2 * gn
    c = pl.program_id(1)

    @pl.when(c == 0)
    def _init():
        xpad_ref[0:SUBLANES, :] = jnp.zeros((SUBLANES, conv_dim), F32)
        if zero_init:
            ht_ref[...] = jnp.zeros(ht_ref.shape, F32)
        else:
            xpad_ref[SUBLANES - (SSD_CONV - 1):SUBLANES, :] = cs_ref[0]
            for g in range(G):
                ht_ref[g] = ss_ref[0, R * g:R * (g + 1)].reshape(GP, SSD_STATE).T

    xpad_ref[SUBLANES:SUBLANES + L, 0:d_inner] = x_ref[0]
    xpad_ref[SUBLANES:SUBLANES + L, d_inner:d_inner + gn] = b_ref[0]
    xpad_ref[SUBLANES:SUBLANES + L, d_inner + gn:conv_dim] = c_ref[0]
    col_chunk = 512
    for j in range(conv_dim // col_chunk):
        cols = slice(j * col_chunk, (j + 1) * col_chunk)
        acc = cb_ref[:, cols] + cw_ref[3:4, cols] * xpad_ref[SUBLANES:SUBLANES + L, cols]
        for tap in range(SSD_CONV - 1):
            off = SUBLANES - (SSD_CONV - 1) + tap
            acc = acc + cw_ref[tap:tap + 1, cols] * xpad_ref[off:off + L, cols]
        act_ref[:, cols] = acc * _sigmoid(acc)

    @pl.when(c == n_chunks - 1)
    def _conv_out():
        cso_ref[0] = xpad_ref[valid + SUBLANES - (SSD_CONV - 1):valid + SUBLANES, :]

    xpad_ref[0:SUBLANES, :] = xpad_ref[L:L + SUBLANES, :]

    dt = _softplus(dt_ref[0] + dtb_ref[...])
    t_io = lax.broadcasted_iota(jnp.int32, (L, L), 0)
    s_io = lax.broadcasted_iota(jnp.int32, (L, L), 1)
    causal = t_io >= s_io
    if valid < L:
        row = lax.broadcasted_iota(jnp.int32, (L, LANES), 0)
        dt = jnp.where(row < valid, dt, 0.0)
    adt = dt * (-jnp.exp(alog_ref[...]))
    tri = jnp.where(causal, 1.0, 0.0).astype(BF16)
    hi = adt.astype(BF16)
    rem = adt - hi.astype(F32)
    mid = rem.astype(BF16)
    lo = (rem - mid.astype(F32)).astype(BF16)
    acs = (jnp.dot(tri, hi, preferred_element_type=F32)
           + jnp.dot(tri, mid, preferred_element_type=F32)
           + jnp.dot(tri, lo, preferred_element_type=F32))
    acs_t = acs.T
    alast = acs[L - 1:L, :]
    e_cs = jnp.exp(acs)
    d_end = jnp.exp(alast - acs)
    e_last = jnp.exp(alast)

    for g in range(G):
        bg = act_ref[:, d_inner + g * SSD_STATE:d_inner + (g + 1) * SSD_STATE]
        cg = act_ref[:, d_inner + gn + g * SSD_STATE:d_inner + gn + (g + 1) * SSD_STATE]
        bb = bg.astype(BF16)
        cbf = cg.astype(BF16)
        cb = lax.dot_general(cbf, bb, (((1,), (1,)), ((), ())), preferred_element_type=F32)
        h_t = ht_ref[g]
        y_off = jnp.dot(cbf, h_t.astype(BF16), preferred_element_type=F32)
        xg = act_ref[:, g * GP:(g + 1) * GP]
        dsk = dsk_ref[:, g * GP:(g + 1) * GP]
        ys, xws, decs = [], [], []
        for r in range(R):
            h = g * R + r
            xh = xg[:, r * P:(r + 1) * P]
            xdt = xh * dt[:, h:h + 1]
            seg = acs[:, h:h + 1] - acs_t[h:h + 1, :]
            dec = jnp.where(causal, jnp.exp(seg), 0.0)
            m = (cb * dec).astype(BF16)
            y_d = jnp.dot(m, xdt.astype(BF16), preferred_element_type=F32)
            y_o = y_off[:, r * P:(r + 1) * P] * e_cs[:, h:h + 1]
            ys.append(y_d + y_o + dsk[:, r * P:(r + 1) * P] * xh)
            xws.append(xdt * d_end[:, h:h + 1])
            decs.append(jnp.broadcast_to(e_last[:, h:h + 1], (1, P)))
        yg = jnp.concatenate(ys, axis=1)
        xw = jnp.concatenate(xws, axis=1)
        zg = z_ref[0, :, g * GP:(g + 1) * GP]
        yg = yg * (zg * _sigmoid(zg))
        y_ref[0, :, g * GP:(g + 1) * GP] = _rms_scale(yg, nw_ref[:, g * GP:(g + 1) * GP]).astype(y_ref.dtype)
        upd = jnp.dot(bg.T.astype(BF16), xw.astype(BF16), preferred_element_type=F32)
        ht_ref[g] = h_t * jnp.concatenate(decs, axis=1) + upd

    @pl.when(c == n_chunks - 1)
    def _state_out():
        for g in range(G):
            sso_ref[0, R * g:R * (g + 1)] = ht_ref[g].T.reshape(R, P, SSD_STATE)


def ssd_scan(zx, conv_state, ssm_state, conv_w, conv_b, dt_bias, a_log, d_skip, norm_w, *, valid):
    b, s, _ = zx.shape
    n_heads = dt_bias.shape[0]
    d_inner = n_heads * SSD_HEADDIM
    gn = SSD_GROUPS * SSD_STATE
    conv_dim = d_inner + 2 * gn
    L = SSD_CHUNK
    n_chunks = s // L
    zero_init = conv_state is None
    pad_h = LANES - n_heads
    consts = [conv_w, conv_b.reshape(1, conv_dim),
              jnp.pad(dt_bias, (0, pad_h)).reshape(1, LANES),
              jnp.pad(a_log, (0, pad_h)).reshape(1, LANES),
              jnp.repeat(d_skip, SSD_HEADDIM).reshape(1, d_inner),
              norm_w.reshape(1, d_inner)]
    const_specs = [pl.BlockSpec(a.shape, lambda i, j: (0, 0)) for a in consts]
    x_blk = d_inner // d_inner
    in_specs = [pl.BlockSpec((1, L, d_inner), lambda i, j: (i, j, 0)),
                pl.BlockSpec((1, L, d_inner), lambda i, j: (i, j, x_blk)),
                pl.BlockSpec((1, L, gn), lambda i, j: (i, j, 2 * d_inner // gn)),
                pl.BlockSpec((1, L, gn), lambda i, j: (i, j, 2 * d_inner // gn + 1)),
                pl.BlockSpec((1, L, LANES), lambda i, j: (i, j, (2 * d_inner + 2 * gn) // LANES)),
                ] + const_specs
    args = [zx, zx, zx, zx, zx] + consts
    if not zero_init:
        in_specs += [pl.BlockSpec((1, SSD_CONV - 1, conv_dim), lambda i, j: (i, 0, 0)),
                     pl.BlockSpec((1, n_heads, SSD_HEADDIM, SSD_STATE), lambda i, j: (i, 0, 0, 0))]
        args += [conv_state, ssm_state]
    out_shape = (jax.ShapeDtypeStruct((b, s, d_inner), BF16),
                 jax.ShapeDtypeStruct((b, SSD_CONV - 1, conv_dim), F32),
                 jax.ShapeDtypeStruct((b, n_heads, SSD_HEADDIM, SSD_STATE), F32))
    out_specs = (pl.BlockSpec((1, L, d_inner), lambda i, j: (i, j, 0)),
                 pl.BlockSpec((1, SSD_CONV - 1, conv_dim), lambda i, j: (i, 0, 0)),
                 pl.BlockSpec((1, n_heads, SSD_HEADDIM, SSD_STATE), lambda i, j: (i, 0, 0, 0)))
    body = functools.partial(_ssd_body, valid=valid, zero_init=zero_init, n_chunks=n_chunks,
                             n_heads=n_heads)
    return pl.pallas_call(
        body,
        grid=(b, n_chunks),
        in_specs=in_specs,
        out_specs=out_specs,
        out_shape=out_shape,
        scratch_shapes=[pltpu.VMEM((L + SUBLANES, conv_dim), F32),
                        pltpu.VMEM((L, conv_dim), F32),
                        pltpu.VMEM((SSD_GROUPS, SSD_STATE, d_inner // SSD_GROUPS), F32)],
        compiler_params=_params("arbitrary", "arbitrary"),
        name="ssd_scan",
    )(*args)


def _qk_norm_body(q_ref, k_ref, gq_ref, gk_ref, avg_ref, qo_ref, ko_ref, *, scale):
    def head_norm(x, g):
        ms = jnp.dot((x * x).astype(BF16), avg_ref[...], preferred_element_type=F32)
        return x * lax.rsqrt(ms + EPS) * g

    qo_ref[...] = (head_norm(q_ref[...], gq_ref[...]) * scale).astype(qo_ref.dtype)
    ko_ref[...] = head_norm(k_ref[...], gk_ref[...])


def qk_norm(qkv, q_gain, k_gain, tm):
    t, n3 = qkv.shape
    d = n3 // 3
    n_heads = d // SB_HEADDIM
    head_of = jnp.arange(d) // SB_HEADDIM
    avg = jnp.where(head_of[:, None] == head_of[None, :], 1.0 / SB_HEADDIM, 0.0).astype(BF16)
    gq = jnp.tile(q_gain, n_heads).reshape(1, d)
    gk = jnp.tile(k_gain, n_heads).reshape(1, d)
    body = functools.partial(_qk_norm_body, scale=SB_HEADDIM ** -0.5)
    return pl.pallas_call(
        body,
        grid=(t // tm,),
        in_specs=[pl.BlockSpec((tm, d), lambda i: (i, 0)),
                  pl.BlockSpec((tm, d), lambda i: (i, 1)),
                  pl.BlockSpec((1, d), lambda i: (0, 0)),
                  pl.BlockSpec((1, d), lambda i: (0, 0)),
                  pl.BlockSpec((d, d), lambda i: (0, 0))],
        out_specs=(pl.BlockSpec((tm, d), lambda i: (i, 0)),
                   pl.BlockSpec((tm, d), lambda i: (i, 0))),
        out_shape=(jax.ShapeDtypeStruct((t, d), BF16), jax.ShapeDtypeStruct((t, d), F32)),
        compiler_params=_params("arbitrary"),
        name="qk_norm",
    )(qkv, qkv, gq, gk, avg)


def _attn_body(q_ref, k_ref, vt_ref, o_ref, *, tq, q_pos0, n_key_blocks):
    nv = VREGS_PER_KEY_BLOCK
    qi = pl.program_id(2)
    q = q_ref[0]
    lane = lax.broadcasted_iota(jnp.int32, (tq, LANES), 1)
    r_io = lax.broadcasted_iota(jnp.int32, (KEY_BLOCK, tq), 0)
    key_in_block = (r_io % SUBLANES) * nv + r_io // SUBLANES
    q_pos = q_pos0 + qi * tq + lax.broadcasted_iota(jnp.int32, (KEY_BLOCK, tq), 1)
    sub = lax.broadcasted_iota(jnp.int32, (SUBLANES, tq), 0)
    n_blk = jnp.minimum(n_key_blocks, (q_pos0 + (qi + 1) * tq - 1 + KEY_BLOCK - 1) // KEY_BLOCK)

    def shift_up(a, k):
        return jnp.where(sub < SUBLANES - k, pltpu.roll(a, SUBLANES - k, 0), 1.0)

    accs = []
    for hh in range(2):
        qm = jnp.where(lane // SB_HEADDIM == hh, q, jnp.zeros_like(q))

        def block(i, carry, qm=qm, hh=hh):
            rest, acc = carry
            j = n_blk - 1 - i
            z = lax.dot_general(k_ref[0, 0, j], qm, (((1,), (1,)), ((), ())),
                                preferred_element_type=F32)
            mask = j * KEY_BLOCK + key_in_block < q_pos
            e = jnp.exp(-jnp.abs(z))
            s = 1.0 / (1.0 + e)
            es = e * s
            pos = z >= 0.0
            beta = jnp.where(mask, jnp.where(pos, s, es), 0.0).reshape(nv, SUBLANES, tq)
            om = jnp.where(mask, jnp.where(pos, es, s), 1.0).reshape(nv, SUBLANES, tq)
            suffix = [None] * nv
            run = om[nv - 1]
            suffix[nv - 1] = run
            for v in range(nv - 2, -1, -1):
                run = om[v] * run
                suffix[v] = run
            row_tot = suffix[0]
            later = shift_up(row_tot, 1)
            later = later * shift_up(later, 1)
            later = later * shift_up(later, 2)
            later = later * shift_up(later, 4)
            f = later * rest
            ws = [beta[v] * suffix[v + 1] * f for v in range(nv - 1)] + [beta[nv - 1] * f]
            w = jnp.stack(ws, axis=0).reshape(KEY_BLOCK, tq).astype(BF16)
            tot = row_tot * later
            rest = rest * jnp.broadcast_to(tot[0:1, :], (SUBLANES, tq))
            vt = vt_ref[0, 0, j]
            acc = acc + jnp.dot(vt[hh * SB_HEADDIM:(hh + 1) * SB_HEADDIM, :], w,
                                preferred_element_type=F32)
            return rest, acc

        init = (jnp.ones((SUBLANES, tq), F32), jnp.zeros((SB_HEADDIM, tq), F32))
        accs.append(lax.fori_loop(0, n_blk, block, init)[1])
    o_ref[0] = jnp.concatenate(accs, axis=0).T.astype(o_ref.dtype)


def _attn_layout(k, v):
    b, s, d = k.shape
    nkb = s // KEY_BLOCK
    nv = VREGS_PER_KEY_BLOCK
    pairs = d // LANES
    kp = k.astype(BF16).reshape(b, nkb, SUBLANES, nv, pairs, LANES)
    kp = kp.transpose(0, 4, 1, 3, 2, 5).reshape(b, pairs, nkb, KEY_BLOCK, LANES)
    vt = v.astype(BF16).reshape(b, nkb, SUBLANES, nv, pairs, LANES)
    vt = vt.transpose(0, 4, 1, 5, 3, 2).reshape(b, pairs, nkb, LANES, KEY_BLOCK)
    return kp, vt


def sb_attention(q, kp, vt, *, tq, q_pos0):
    b, sq, d = q.shape
    pairs, nkb = kp.shape[1], kp.shape[2]
    body = functools.partial(_attn_body, tq=tq, q_pos0=q_pos0, n_key_blocks=nkb)
    return pl.pallas_call(
        body,
        grid=(b, pairs, sq // tq),
        in_specs=[pl.BlockSpec((1, tq, LANES), lambda i, p, t: (i, t, p)),
                  pl.BlockSpec((1, 1, nkb, KEY_BLOCK, LANES), lambda i, p, t: (i, p, 0, 0, 0)),
                  pl.BlockSpec((1, 1, nkb, LANES, KEY_BLOCK), lambda i, p, t: (i, p, 0, 0, 0))],
        out_specs=pl.BlockSpec((1, tq, LANES), lambda i, p, t: (i, t, p)),
        out_shape=jax.ShapeDtypeStruct((b, sq, d), BF16),
        compiler_params=_params("arbitrary", "arbitrary", "arbitrary"),
        name="sb_attention",
    )(q, kp, vt)


def _row_tile(t):
    return 1024 if t % 1024 == 0 else t


def _ssd_layer(y, b, s, conv_state, ssm_state, g_mix, in_w, conv_w, conv_b, dt_bias, a_log, d_skip,
               norm_w, out_w):
    t, d = y.shape
    n_in = in_w.shape[1]
    tn = 1280
    n_pad = -(-n_in // tn) * tn
    w_in = jnp.pad(in_w, ((0, 0), (0, n_pad - n_in))).astype(BF16)
    zx = norm_matmul(y, g_mix, w_in, _row_tile(t), tn).reshape(b, s, n_pad)
    valid = SSD_CHUNK
    if s % SSD_CHUNK:
        assert s < SSD_CHUNK
        valid = s
        zx = jnp.pad(zx, ((0, 0), (0, SSD_CHUNK - s), (0, 0)))
    y_ssd, conv_out, ssm_out = ssd_scan(zx, conv_state, ssm_state, conv_w, conv_b, dt_bias, a_log,
                                        d_skip, norm_w, valid=valid)
    y_ssd = y_ssd[:, :s].reshape(t, -1)
    return matmul_res(y_ssd, out_w.astype(BF16), y, _row_tile(t)), conv_out, ssm_out


def _sb_layer(y, b, s, cache_k, cache_v, g_mix, qkv_w, q_gain, k_gain, out_w):
    t, d = y.shape
    tm = _row_tile(t)
    qkv = norm_matmul(y, g_mix, qkv_w.astype(BF16), tm, 1024)
    q, k = qk_norm(qkv, q_gain, k_gain, tm)
    v = qkv[:, 2 * d:]
    q = q.reshape(b, s, d)
    k3 = k.reshape(b, s, d)
    v3 = v.reshape(b, s, d)
    if cache_k is None:
        tq, q_pos0, k_all, v_all = 256, 0, k3, v3
    else:
        past = cache_k.shape[1]
        tq, q_pos0 = KEY_BLOCK, past
        k_all = jnp.concatenate([cache_k.reshape(b, past, d), k3], axis=1)
        v_all = jnp.concatenate([cache_v.reshape(b, past, d), v3], axis=1)
        pad_k = -(past + s) % KEY_BLOCK
        k_all = jnp.pad(k_all, ((0, 0), (0, pad_k), (0, 0)))
        v_all = jnp.pad(v_all, ((0, 0), (0, pad_k), (0, 0)))
        q = jnp.pad(q, ((0, 0), (0, tq - s), (0, 0)))
    kp, vt = _attn_layout(k_all, v_all)
    o = sb_attention(q, kp, vt, tq=tq, q_pos0=q_pos0)[:, :s].reshape(t, d)
    heads = d // SB_HEADDIM
    return (matmul_res(o, out_w.astype(BF16), y, tm),
            k.reshape(b, s, heads, SB_HEADDIM), v.reshape(b, s, heads, SB_HEADDIM))


def kernel(x_prompt, x_sample, state_ssm, state_conv, cache_k, cache_v, norm_mix, norm_mlp, ssd_in_w, ssd_conv_w, ssd_conv_b, ssd_dt_bias, ssd_a_log, ssd_d, ssd_norm_w, ssd_out_w, sb_qkv_w, sb_q_gain, sb_k_gain, sb_out_w, mlp_up, mlp_down):
    bp, sp, d = x_prompt.shape
    bs, ss, _ = x_sample.shape
    depth = norm_mix.shape[0]
    y_p = x_prompt.reshape(bp * sp, d)
    y_s = x_sample.reshape(bs * ss, d)
    p_ssm, p_conv, p_k, p_v = [], [], [], []
    s_ssm, s_conv, s_k, s_v = [], [], [], []
    for i in range(depth):
        j = i // 2
        if i % 2 == 0:
            prm = (norm_mix[i], ssd_in_w[j], ssd_conv_w[j], ssd_conv_b[j], ssd_dt_bias[j],
                   ssd_a_log[j], ssd_d[j], ssd_norm_w[j], ssd_out_w[j])
            y_p, c_p, st_p = _ssd_layer(y_p, bp, sp, None, None, *prm)
            y_s, c_s, st_s = _ssd_layer(y_s, bs, ss, state_conv[j], state_ssm[j], *prm)
            p_ssm.append(st_p); p_conv.append(c_p)
            s_ssm.append(st_s); s_conv.append(c_s)
        else:
            prm = (norm_mix[i], sb_qkv_w[j], sb_q_gain[j], sb_k_gain[j], sb_out_w[j])
            y_p, k_p, v_p = _sb_layer(y_p, bp, sp, None, None, *prm)
            y_s, k_s, v_s = _sb_layer(y_s, bs, ss, cache_k[j], cache_v[j], *prm)
            p_k.append(k_p); p_v.append(v_p)
            s_k.append(k_s); s_v.append(v_s)
        wu = mlp_up[i].astype(BF16)
        wd = mlp_down[i].astype(BF16)
        y_p = mlp(y_p, norm_mlp[i], wu, wd, _row_tile(y_p.shape[0]), 1024)
        y_s = mlp(y_s, norm_mlp[i], wu, wd, _row_tile(y_s.shape[0]), 1024)
    return (y_p.reshape(bp, sp, d), y_s.reshape(bs, ss, d),
            jnp.stack(p_ssm), jnp.stack(p_conv), jnp.stack(p_k), jnp.stack(p_v),
            jnp.stack(s_ssm), jnp.stack(s_conv), jnp.stack(s_k), jnp.stack(s_v))
```

```python
import functools

import jax
import jax.numpy as jnp
from jax import lax
from jax.experimental import pallas as pl
from jax.experimental.pallas import tpu as pltpu

F32 = jnp.float32
BF16 = jnp.bfloat16
EPS = 1e-6

LANES = 128
SUBLANES = 8
VMEM_LIMIT_BYTES = 56 * 1024 * 1024

SSD_HEADDIM = 64
SSD_GROUPS = 8
SSD_STATE = 128
SSD_CONV = 4
SB_HEADDIM = 64
SSD_CHUNK = 128
KEY_BLOCK = 128
VREGS_PER_KEY_BLOCK = KEY_BLOCK // SUBLANES
KEY_BLOCKS_PER_STEP = 4


def _params(*sem):
    return pltpu.CompilerParams(dimension_semantics=sem, vmem_limit_bytes=VMEM_LIMIT_BYTES)


def _rms_scale(x, g):
    ms = jnp.mean(x * x, axis=-1, keepdims=True)
    return x * lax.rsqrt(ms + EPS) * g


def _silu(x):
    h = 0.5 * x
    return h + h * jnp.tanh(h)


def _softplus(x):
    return jnp.maximum(x, 0.0) + jnp.log(1.0 + jnp.exp(-jnp.abs(x)))


def _split3(a):
    hi = a.astype(BF16)
    rem = a - hi.astype(F32)
    mid = rem.astype(BF16)
    lo = (rem - mid.astype(F32)).astype(BF16)
    return hi, mid, lo


def _dot_f32(a, b):
    return jnp.dot(a, b, preferred_element_type=F32)


def _norm_matmul_body(x_ref, g_ref, w_ref, o_ref, xn_ref):
    @pl.when(pl.program_id(1) == 0)
    def _():
        xn_ref[...] = _rms_scale(x_ref[...], g_ref[...]).astype(BF16)

    o_ref[...] = _dot_f32(xn_ref[...], w_ref[...])


def norm_matmul(x, g, w, tm, tn):
    t, d = x.shape
    n = w.shape[1]
    return pl.pallas_call(
        _norm_matmul_body,
        grid=(t // tm, n // tn),
        in_specs=[pl.BlockSpec((tm, d), lambda i, j: (i, 0)),
                  pl.BlockSpec((1, d), lambda i, j: (0, 0)),
                  pl.BlockSpec((d, tn), lambda i, j: (0, j))],
        out_specs=pl.BlockSpec((tm, tn), lambda i, j: (i, j)),
        out_shape=jax.ShapeDtypeStruct((t, n), F32),
        scratch_shapes=[pltpu.VMEM((tm, d), BF16)],
        compiler_params=_params("arbitrary", "arbitrary"),
        name="norm_matmul",
    )(x, g.reshape(1, d), w)


def _matmul_res_body(a_ref, w_ref, r_ref, o_ref):
    o_ref[...] = r_ref[...] + _dot_f32(a_ref[...].astype(BF16), w_ref[...])


def matmul_res(a, w, res, tm):
    t, k = a.shape
    n = w.shape[1]
    return pl.pallas_call(
        _matmul_res_body,
        grid=(t // tm,),
        in_specs=[pl.BlockSpec((tm, k), lambda i: (i, 0)),
                  pl.BlockSpec((k, n), lambda i: (0, 0)),
                  pl.BlockSpec((tm, n), lambda i: (i, 0))],
        out_specs=pl.BlockSpec((tm, n), lambda i: (i, 0)),
        out_shape=jax.ShapeDtypeStruct((t, n), F32),
        compiler_params=_params("arbitrary"),
        name="matmul_res",
    )(a, w, res)


def _mlp_body(x_ref, g_ref, wu_ref, wd_ref, o_ref, xn_ref):
    @pl.when(pl.program_id(1) == 0)
    def _():
        x = x_ref[...]
        xn_ref[...] = _rms_scale(x, g_ref[...]).astype(BF16)
        o_ref[...] = x

    h = jnp.maximum(_dot_f32(xn_ref[...], wu_ref[...]), 0.0)
    o_ref[...] += _dot_f32((h * h).astype(BF16), wd_ref[...])


def mlp(x, g, wu, wd, tm, tf):
    t, d = x.shape
    f = wu.shape[1]
    return pl.pallas_call(
        _mlp_body,
        grid=(t // tm, f // tf),
        in_specs=[pl.BlockSpec((tm, d), lambda i, j: (i, 0)),
                  pl.BlockSpec((1, d), lambda i, j: (0, 0)),
                  pl.BlockSpec((d, tf), lambda i, j: (0, j)),
                  pl.BlockSpec((tf, d), lambda i, j: (j, 0))],
        out_specs=pl.BlockSpec((tm, d), lambda i, j: (i, 0)),
        out_shape=jax.ShapeDtypeStruct((t, d), F32),
        scratch_shapes=[pltpu.VMEM((tm, d), BF16)],
        compiler_params=_params("arbitrary", "arbitrary"),
        name="mlp",
    )(x, g.reshape(1, d), wu, wd)


def _ssd_body(*refs, valid, zero_init, n_chunks, n_heads):
    if zero_init:
        (z_ref, x_ref, b_ref, c_ref, dt_ref, cw_ref, cb_ref, dtb_ref, alog_ref, dsk_ref, nw_ref,
         hx_ref, y_ref, cso_ref, sso_ref, xpad_ref, act_ref, ht_ref) = refs
        cs_ref = ss_ref = None
    else:
        (z_ref, x_ref, b_ref, c_ref, dt_ref, cw_ref, cb_ref, dtb_ref, alog_ref, dsk_ref, nw_ref,
         hx_ref, cs_ref, ss_ref, y_ref, cso_ref, sso_ref, xpad_ref, act_ref, ht_ref) = refs
    L = SSD_CHUNK
    G = SSD_GROUPS
    R = n_heads // G
    P = SSD_HEADDIM
    GP = R * P
    d_inner = n_heads * P
    gn = G * SSD_STATE
    conv_dim = d_inner + 2 * gn
    c = pl.program_id(1)

    @pl.when(c == 0)
    def _init():
        xpad_ref[0:SUBLANES, :] = jnp.zeros((SUBLANES, conv_dim), F32)
        if zero_init:
            ht_ref[...] = jnp.zeros(ht_ref.shape, F32)
        else:
            xpad_ref[SUBLANES - (SSD_CONV - 1):SUBLANES, :] = cs_ref[0]
            for g in range(G):
                ht_ref[g] = ss_ref[0, R * g:R * (g + 1)].reshape(GP, SSD_STATE).T

    xpad_ref[SUBLANES:SUBLANES + L, 0:d_inner] = x_ref[0]
    xpad_ref[SUBLANES:SUBLANES + L, d_inner:d_inner + gn] = b_ref[0]
    xpad_ref[SUBLANES:SUBLANES + L, d_inner + gn:conv_dim] = c_ref[0]
    col_chunk = 512
    for j in range(conv_dim // col_chunk):
        cols = slice(j * col_chunk, (j + 1) * col_chunk)
        xp = xpad_ref[:, cols]
        acc = cb_ref[:, cols] + cw_ref[SSD_CONV - 1:SSD_CONV, cols] * xp[SUBLANES:]
        for k in range(1, SSD_CONV):
            acc = acc + cw_ref[SSD_CONV - 1 - k:SSD_CONV - k, cols] * pltpu.roll(xp, k, 0)[SUBLANES:]
        act_ref[:, cols] = _silu(acc)

    @pl.when(c == n_chunks - 1)
    def _conv_out():
        cso_ref[0] = xpad_ref[valid + SUBLANES - (SSD_CONV - 1):valid + SUBLANES, :]

    xpad_ref[0:SUBLANES, :] = xpad_ref[L:L + SUBLANES, :]

    dt = _softplus(dt_ref[0] + dtb_ref[...])
    t_io = lax.broadcasted_iota(jnp.int32, (L, L), 0)
    s_io = lax.broadcasted_iota(jnp.int32, (L, L), 1)
    causal = t_io >= s_io
    if valid < L:
        row = lax.broadcasted_iota(jnp.int32, (L, LANES), 0)
        dt = jnp.where(row < valid, dt, 0.0)
    adt = dt * (-jnp.exp(alog_ref[...]))
    tri = jnp.where(causal, 1.0, 0.0).astype(BF16)
    acs = sum(_dot_f32(tri, p) for p in _split3(adt))
    hx = hx_ref[...]
    e_x = jnp.exp(sum(_dot_f32(p, hx) for p in _split3(acs)))
    alast = acs[L - 1:L, :]
    acs_t = acs.T
    src_t = (acs - jnp.log(dt)).T
    u_t = (dt * jnp.exp(alast - acs)).T
    head_of_lane = (lax.broadcasted_iota(jnp.int32, (L, GP), 1) // P).astype(F32).astype(BF16)

    for g in range(G):
        gc = slice(g * GP, (g + 1) * GP)
        bg = act_ref[:, d_inner + g * SSD_STATE:d_inner + (g + 1) * SSD_STATE]
        cg = act_ref[:, d_inner + gn + g * SSD_STATE:d_inner + gn + (g + 1) * SSD_STATE]
        cbf = cg.astype(BF16)
        cb = lax.dot_general(cbf, bg.astype(BF16), (((1,), (1,)), ((), ())),
                             preferred_element_type=F32)
        bg_t = bg.T
        h_t = ht_ref[g]
        y_off = _dot_f32(cbf, h_t.astype(BF16)) * e_x[:, gc]
        xg = act_ref[:, gc]
        xb = xg.astype(BF16)
        x_bd = jnp.concatenate([jnp.where(head_of_lane == r, xb, jnp.zeros_like(xb))
                                for r in range(R)], axis=0)
        ms, bs = [], []
        for r in range(R):
            h = g * R + r
            seg = acs[:, h:h + 1] - src_t[h:h + 1, :]
            ms.append((jnp.where(causal, jnp.exp(seg), 0.0) * cb).astype(BF16))
            bs.append((bg_t * u_t[h:h + 1, :]).astype(BF16))
        y_d = _dot_f32(jnp.concatenate(ms, axis=1), x_bd)
        upd = _dot_f32(jnp.concatenate(bs, axis=1), x_bd)
        yg = (y_d + y_off + dsk_ref[:, gc] * xg) * _silu(z_ref[0, :, gc])
        y_ref[0, :, gc] = _rms_scale(yg, nw_ref[:, gc]).astype(y_ref.dtype)
        ht_ref[g] = h_t * e_x[L - 1:L, gc] + upd

    @pl.when(c == n_chunks - 1)
    def _state_out():
        for g in range(G):
            sso_ref[0, R * g:R * (g + 1)] = ht_ref[g].T.reshape(R, P, SSD_STATE)


def ssd_scan(zx, conv_state, ssm_state, conv_w, conv_b, dt_bias, a_log, d_skip, norm_w, *, valid):
    b, s, _ = zx.shape
    n_heads = dt_bias.shape[0]
    d_inner = n_heads * SSD_HEADDIM
    gn = SSD_GROUPS * SSD_STATE
    conv_dim = d_inner + 2 * gn
    L = SSD_CHUNK
    n_chunks = s // L
    zero_init = conv_state is None
    pad_h = LANES - n_heads
    head_expand = (jnp.arange(LANES)[:, None] == jnp.arange(d_inner)[None, :] // SSD_HEADDIM)
    consts = [conv_w, conv_b.reshape(1, conv_dim),
              jnp.pad(dt_bias, (0, pad_h)).reshape(1, LANES),
              jnp.pad(a_log, (0, pad_h)).reshape(1, LANES),
              jnp.repeat(d_skip, SSD_HEADDIM).reshape(1, d_inner),
              norm_w.reshape(1, d_inner),
              head_expand.astype(BF16)]
    const_specs = [pl.BlockSpec(a.shape, lambda i, j: (0, 0)) for a in consts]
    in_specs = [pl.BlockSpec((1, L, d_inner), lambda i, j: (i, j, 0)),
                pl.BlockSpec((1, L, d_inner), lambda i, j: (i, j, 1)),
                pl.BlockSpec((1, L, gn), lambda i, j: (i, j, 2 * d_inner // gn)),
                pl.BlockSpec((1, L, gn), lambda i, j: (i, j, 2 * d_inner // gn + 1)),
                pl.BlockSpec((1, L, LANES), lambda i, j: (i, j, (2 * d_inner + 2 * gn) // LANES)),
                ] + const_specs
    args = [zx, zx, zx, zx, zx] + consts
    if not zero_init:
        in_specs += [pl.BlockSpec((1, SSD_CONV - 1, conv_dim), lambda i, j: (i, 0, 0)),
                     pl.BlockSpec((1, n_heads, SSD_HEADDIM, SSD_STATE), lambda i, j: (i, 0, 0, 0))]
        args += [conv_state, ssm_state]
    out_shape = (jax.ShapeDtypeStruct((b, s, d_inner), BF16),
                 jax.ShapeDtypeStruct((b, SSD_CONV - 1, conv_dim), F32),
                 jax.ShapeDtypeStruct((b, n_heads, SSD_HEADDIM, SSD_STATE), F32))
    out_specs = (pl.BlockSpec((1, L, d_inner), lambda i, j: (i, j, 0)),
                 pl.BlockSpec((1, SSD_CONV - 1, conv_dim), lambda i, j: (i, 0, 0)),
                 pl.BlockSpec((1, n_heads, SSD_HEADDIM, SSD_STATE), lambda i, j: (i, 0, 0, 0)))
    body = functools.partial(_ssd_body, valid=valid, zero_init=zero_init, n_chunks=n_chunks,
                             n_heads=n_heads)
    return pl.pallas_call(
        body,
        grid=(b, n_chunks),
        in_specs=in_specs,
        out_specs=out_specs,
        out_shape=out_shape,
        scratch_shapes=[pltpu.VMEM((L + SUBLANES, conv_dim), F32),
                        pltpu.VMEM((L, conv_dim), F32),
                        pltpu.VMEM((SSD_GROUPS, SSD_STATE, d_inner // SSD_GROUPS), F32)],
        compiler_params=_params("arbitrary", "arbitrary"),
        name="ssd_scan",
    )(*args)


def _qk_norm_body(q_ref, k_ref, v_ref, gq_ref, gk_ref, avg_ref, qo_ref, ko_ref, vo_ref, *, scale):
    def head_norm(x, g):
        ms = _dot_f32((x * x).astype(BF16), avg_ref[...])
        return x * lax.rsqrt(ms + EPS) * g

    qo_ref[...] = (head_norm(q_ref[...], gq_ref[...]) * scale).astype(qo_ref.dtype)
    ko_ref[...] = head_norm(k_ref[...], gk_ref[...])
    vo_ref[...] = v_ref[...]


def qk_norm(qkv, q_gain, k_gain, tm):
    t, n3 = qkv.shape
    d = n3 // 3
    n_heads = d // SB_HEADDIM
    head_of = jnp.arange(d) // SB_HEADDIM
    avg = jnp.where(head_of[:, None] == head_of[None, :], 1.0 / SB_HEADDIM, 0.0).astype(BF16)
    gq = jnp.tile(q_gain, n_heads).reshape(1, d)
    gk = jnp.tile(k_gain, n_heads).reshape(1, d)
    body = functools.partial(_qk_norm_body, scale=0.5 * SB_HEADDIM ** -0.5)
    row_spec = pl.BlockSpec((tm, d), lambda i: (i, 0))
    return pl.pallas_call(
        body,
        grid=(t // tm,),
        in_specs=[pl.BlockSpec((tm, d), lambda i: (i, 0)),
                  pl.BlockSpec((tm, d), lambda i: (i, 1)),
                  pl.BlockSpec((tm, d), lambda i: (i, 2)),
                  pl.BlockSpec((1, d), lambda i: (0, 0)),
                  pl.BlockSpec((1, d), lambda i: (0, 0)),
                  pl.BlockSpec((d, d), lambda i: (0, 0))],
        out_specs=(row_spec, row_spec, row_spec),
        out_shape=(jax.ShapeDtypeStruct((t, d), BF16), jax.ShapeDtypeStruct((t, d), F32),
                   jax.ShapeDtypeStruct((t, d), F32)),
        compiler_params=_params("arbitrary"),
        name="qk_norm",
    )(qkv, qkv, qkv, gq, gk, avg)


def _kv_layout_body(k_ref, v_ref, perm_ref, kp_ref, vt_ref, *, blocks):
    pairs = kp_ref.shape[1]
    perm = perm_ref[...]
    for blk in range(blocks):
        rows = slice(blk * KEY_BLOCK, (blk + 1) * KEY_BLOCK)
        kp = _dot_f32(perm, k_ref[0, rows, :].astype(BF16))
        vp = _dot_f32(perm, v_ref[0, rows, :].astype(BF16))
        for p in range(pairs):
            lanes = slice(p * LANES, (p + 1) * LANES)
            kp_ref[0, p, blk] = kp[:, lanes].astype(BF16)
            vt_ref[0, p, blk] = vp[:, lanes].T.astype(BF16)


def kv_layout(k, v, v_col):
    b, s, d = k.shape
    nkb = s // KEY_BLOCK
    blocks = 8 if nkb % 8 == 0 else nkb
    pairs = d // LANES
    r = jnp.arange(KEY_BLOCK)
    key_of_row = (r % SUBLANES) * VREGS_PER_KEY_BLOCK + r // SUBLANES
    perm = (key_of_row[:, None] == r[None, :]).astype(BF16)
    out = jax.ShapeDtypeStruct((b, pairs, nkb, KEY_BLOCK, LANES), BF16)
    out_spec = pl.BlockSpec((1, pairs, blocks, KEY_BLOCK, LANES), lambda i, j: (i, 0, j, 0, 0))
    return pl.pallas_call(
        functools.partial(_kv_layout_body, blocks=blocks),
        grid=(b, nkb // blocks),
        in_specs=[pl.BlockSpec((1, blocks * KEY_BLOCK, d), lambda i, j: (i, j, 0)),
                  pl.BlockSpec((1, blocks * KEY_BLOCK, d), lambda i, j: (i, j, v_col)),
                  pl.BlockSpec((KEY_BLOCK, KEY_BLOCK), lambda i, j: (0, 0))],
        out_specs=(out_spec, out_spec),
        out_shape=(out, out),
        compiler_params=_params("arbitrary", "arbitrary"),
        name="kv_layout",
    )(k, v, perm)


def _attn_body(q_ref, k_ref, vt_ref, o_ref, *, tq, q_pos0):
    nv = VREGS_PER_KEY_BLOCK
    qi = pl.program_id(2)
    q = q_ref[0]
    lane = lax.broadcasted_iota(jnp.int32, (tq, LANES), 1)
    qms = [jnp.where(lane // SB_HEADDIM == hh, q, jnp.zeros_like(q)) for hh in range(2)]
    r_io = lax.broadcasted_iota(jnp.int32, (KEY_BLOCK, KEY_BLOCK), 0)
    diag_mask = ((r_io % SUBLANES) * nv + r_io // SUBLANES
                 < lax.broadcasted_iota(jnp.int32, (KEY_BLOCK, KEY_BLOCK), 1))
    n_diag = tq // KEY_BLOCK
    n_clear = (q_pos0 + qi * tq) // KEY_BLOCK

    def shift_up(a, k):
        sub = lax.broadcasted_iota(jnp.int32, a.shape, 0)
        return jnp.where(sub < SUBLANES - k, pltpu.roll(a, SUBLANES - k, 0), 1.0)

    def group(j0, n, carry, diag):
        out = []
        for hh in range(2):
            rest, acc = carry[hh]
            kb = k_ref[0, 0, pl.ds(j0, n)].reshape(n * KEY_BLOCK, LANES)
            z = lax.dot_general(kb, qms[hh], (((1,), (1,)), ((), ())),
                                preferred_element_type=F32)
            ws = [None] * n
            for r in range(n - 1, -1, -1):
                lo = r * KEY_BLOCK if diag else 0
                width = tq - lo
                half_t = 0.5 * jnp.tanh(z[r * KEY_BLOCK:(r + 1) * KEY_BLOCK, lo:])
                beta = half_t + 0.5
                om = 0.5 - half_t
                if diag:
                    beta_d = jnp.where(diag_mask, beta[:, :KEY_BLOCK], 0.0)
                    om_d = jnp.where(diag_mask, om[:, :KEY_BLOCK], 1.0)
                    if width > KEY_BLOCK:
                        beta = jnp.concatenate([beta_d, beta[:, KEY_BLOCK:]], axis=1)
                        om = jnp.concatenate([om_d, om[:, KEY_BLOCK:]], axis=1)
                    else:
                        beta, om = beta_d, om_d
                beta = beta.reshape(nv, SUBLANES, width)
                om = om.reshape(nv, SUBLANES, width)
                suffix = [None] * nv
                run = om[nv - 1]
                suffix[nv - 1] = run
                for v in range(nv - 2, -1, -1):
                    run = om[v] * run
                    suffix[v] = run
                row_tot = suffix[0]
                later = shift_up(row_tot, 1)
                later = later * shift_up(later, 1)
                later = later * shift_up(later, 2)
                later = later * shift_up(later, 4)
                rest_r = rest[:, lo:]
                f = later * rest_r
                w = jnp.stack([beta[v] * (suffix[v + 1] * f) for v in range(nv - 1)]
                              + [beta[nv - 1] * f], axis=0).reshape(KEY_BLOCK, width).astype(BF16)
                tot = jnp.broadcast_to((row_tot * later)[0:1, :], (SUBLANES, width))
                if lo:
                    w = jnp.concatenate([jnp.zeros((KEY_BLOCK, lo), BF16), w], axis=1)
                    rest = jnp.concatenate([rest[:, :lo], rest_r * tot], axis=1)
                else:
                    rest = rest_r * tot
                ws[r] = w
            vt = jnp.concatenate([vt_ref[0, 0, j0 + r][hh * SB_HEADDIM:(hh + 1) * SB_HEADDIM, :]
                                  for r in range(n)], axis=1)
            acc = acc + _dot_f32(vt, jnp.concatenate(ws, axis=0))
            out.append((rest, acc))
        return tuple(out)

    carry = tuple((jnp.ones((SUBLANES, tq), F32), jnp.zeros((SB_HEADDIM, tq), F32)) for _ in range(2))
    carry = group(n_clear, n_diag, carry, True)

    def far(i, carry):
        return group(n_clear - KEY_BLOCKS_PER_STEP * (i + 1), KEY_BLOCKS_PER_STEP, carry, False)

    carry = lax.fori_loop(0, n_clear // KEY_BLOCKS_PER_STEP, far, carry)
    o_ref[0] = jnp.concatenate([carry[0][1], carry[1][1]], axis=0).T.astype(o_ref.dtype)


def sb_attention(q, kp, vt, *, tq, q_pos0):
    b, sq, d = q.shape
    pairs, nkb = kp.shape[1], kp.shape[2]
    n_tiles = sq // tq
    for t in range(n_tiles):
        assert (q_pos0 + t * tq) % (KEY_BLOCK * KEY_BLOCKS_PER_STEP) == 0
    assert q_pos0 + sq <= nkb * KEY_BLOCK
    body = functools.partial(_attn_body, tq=tq, q_pos0=q_pos0)
    return pl.pallas_call(
        body,
        grid=(b, pairs, n_tiles),
        in_specs=[pl.BlockSpec((1, tq, LANES), lambda i, p, t: (i, t, p)),
                  pl.BlockSpec((1, 1, nkb, KEY_BLOCK, LANES), lambda i, p, t: (i, p, 0, 0, 0)),
                  pl.BlockSpec((1, 1, nkb, LANES, KEY_BLOCK), lambda i, p, t: (i, p, 0, 0, 0))],
        out_specs=pl.BlockSpec((1, tq, LANES), lambda i, p, t: (i, t, p)),
        out_shape=jax.ShapeDtypeStruct((b, sq, d), BF16),
        compiler_params=_params("arbitrary", "arbitrary", "arbitrary"),
        name="sb_attention",
    )(q, kp, vt)


def _row_tile(t):
    return 1024 if t % 1024 == 0 else t


def _ssd_layer(y, b, s, conv_state, ssm_state, g_mix, in_w, conv_w, conv_b, dt_bias, a_log, d_skip,
               norm_w, out_w):
    t, d = y.shape
    n_in = in_w.shape[1]
    tn = 1280
    n_pad = -(-n_in // tn) * tn
    w_in = jnp.pad(in_w, ((0, 0), (0, n_pad - n_in))).astype(BF16)
    zx = norm_matmul(y, g_mix, w_in, _row_tile(t), tn).reshape(b, s, n_pad)
    valid = SSD_CHUNK
    if s % SSD_CHUNK:
        assert s < SSD_CHUNK
        valid = s
        zx = jnp.pad(zx, ((0, 0), (0, SSD_CHUNK - s), (0, 0)))
    y_ssd, conv_out, ssm_out = ssd_scan(zx, conv_state, ssm_state, conv_w, conv_b, dt_bias, a_log,
                                        d_skip, norm_w, valid=valid)
    y_ssd = y_ssd[:, :s].reshape(t, -1)
    return matmul_res(y_ssd, out_w.astype(BF16), y, _row_tile(t)), conv_out, ssm_out


def _sb_layer(y, b, s, cache_k, cache_v, g_mix, qkv_w, q_gain, k_gain, out_w):
    t, d = y.shape
    tm = _row_tile(t)
    qkv = norm_matmul(y, g_mix, qkv_w.astype(BF16), tm, 1024)
    q, k, v = qk_norm(qkv, q_gain, k_gain, tm)
    q = q.reshape(b, s, d)
    if cache_k is None:
        tq, q_pos0 = KEY_BLOCKS_PER_STEP * KEY_BLOCK, 0
        kp, vt = kv_layout(k.reshape(b, s, d), qkv.reshape(b, s, 3 * d), 2)
    else:
        past = cache_k.shape[1]
        tq, q_pos0 = KEY_BLOCK, past
        pad = ((0, 0), (0, tq - s), (0, 0))
        kp_c, vt_c = kv_layout(cache_k.reshape(b, past, d), cache_v.reshape(b, past, d), 0)
        kp_n, vt_n = kv_layout(jnp.pad(k.reshape(b, s, d), pad), jnp.pad(v.reshape(b, s, d), pad), 0)
        kp = jnp.concatenate([kp_c, kp_n], axis=2)
        vt = jnp.concatenate([vt_c, vt_n], axis=2)
        q = jnp.pad(q, pad)
    o = sb_attention(q, kp, vt, tq=tq, q_pos0=q_pos0)[:, :s].reshape(t, d)
    heads = d // SB_HEADDIM
    return (matmul_res(o, out_w.astype(BF16), y, tm),
            k.reshape(b, s, heads, SB_HEADDIM), v.reshape(b, s, heads, SB_HEADDIM))


def kernel(x_prompt, x_sample, state_ssm, state_conv, cache_k, cache_v, norm_mix, norm_mlp, ssd_in_w, ssd_conv_w, ssd_conv_b, ssd_dt_bias, ssd_a_log, ssd_d, ssd_norm_w, ssd_out_w, sb_qkv_w, sb_q_gain, sb_k_gain, sb_out_w, mlp_up, mlp_down):
    bp, sp, d = x_prompt.shape
    bs, ss, _ = x_sample.shape
    depth = norm_mix.shape[0]
    y_p = x_prompt.reshape(bp * sp, d)
    y_s = x_sample.reshape(bs * ss, d)
    p_ssm, p_conv, p_k, p_v = [], [], [], []
    s_ssm, s_conv, s_k, s_v = [], [], [], []
    for i in range(depth):
        j = i // 2
        if i % 2 == 0:
            prm = (norm_mix[i], ssd_in_w[j], ssd_conv_w[j], ssd_conv_b[j], ssd_dt_bias[j],
                   ssd_a_log[j], ssd_d[j], ssd_norm_w[j], ssd_out_w[j])
            y_p, c_p, st_p = _ssd_layer(y_p, bp, sp, None, None, *prm)
            y_s, c_s, st_s = _ssd_layer(y_s, bs, ss, state_conv[j], state_ssm[j], *prm)
            p_ssm.append(st_p); p_conv.append(c_p)
            s_ssm.append(st_s); s_conv.append(c_s)
        else:
            prm = (norm_mix[i], sb_qkv_w[j], sb_q_gain[j], sb_k_gain[j], sb_out_w[j])
            y_p, k_p, v_p = _sb_layer(y_p, bp, sp, None, None, *prm)
            y_s, k_s, v_s = _sb_layer(y_s, bs, ss, cache_k[j], cache_v[j], *prm)
            p_k.append(k_p); p_v.append(v_p)
            s_k.append(k_s); s_v.append(v_s)
        wu = mlp_up[i].astype(BF16)
        wd = mlp_down[i].astype(BF16)
        y_p = mlp(y_p, norm_mlp[i], wu, wd, _row_tile(y_p.shape[0]), 1024)
        y_s = mlp(y_s, norm_mlp[i], wu, wd, _row_tile(y_s.shape[0]), 1024)
    return (y_p.reshape(bp, sp, d), y_s.reshape(bs, ss, d),
            jnp.stack(p_ssm), jnp.stack(p_conv), jnp.stack(p_k), jnp.stack(p_v),
            jnp.stack(s_ssm), jnp.stack(s_conv), jnp.stack(s_k), jnp.stack(s_v))
```

```python
import functools

import jax
import jax.numpy as jnp
from jax import lax
from jax.experimental import pallas as pl
from jax.experimental.pallas import tpu as pltpu

F32 = jnp.float32
BF16 = jnp.bfloat16
EPS = 1e-6

LANES = 128
SUBLANES = 8
MXU_WIDTH = 256
VMEM_LIMIT_BYTES = 56 * 1024 * 1024

SSD_HEADDIM = 64
SSD_GROUPS = 8
SSD_STATE = 128
SSD_CONV = 4
SB_HEADDIM = 64
SSD_CHUNK = 128
KEY_BLOCK = 128
VREGS_PER_KEY_BLOCK = KEY_BLOCK // SUBLANES
KEY_BLOCKS_PER_STEP = 4


def _params(*sem):
    return pltpu.CompilerParams(dimension_semantics=sem, vmem_limit_bytes=VMEM_LIMIT_BYTES)


def _rms_scale(x, g):
    ms = jnp.mean(x * x, axis=-1, keepdims=True)
    return x * lax.rsqrt(ms + EPS) * g


def _silu(x):
    h = 0.5 * x
    return h + h * jnp.tanh(h)


def _softplus(x):
    return jnp.maximum(x, 0.0) + jnp.log(1.0 + jnp.exp(-jnp.abs(x)))


def _split3(a):
    hi = a.astype(BF16)
    rem = a - hi.astype(F32)
    mid = rem.astype(BF16)
    lo = (rem - mid.astype(F32)).astype(BF16)
    return hi, mid, lo


def _dot_f32(a, b):
    return jnp.dot(a, b, preferred_element_type=F32)


def _norm_matmul_body(x_ref, g_ref, w_ref, *rest):
    side = len(rest) == 4
    if side:
        ws_ref, o_ref, os_ref, xn_ref = rest
    else:
        o_ref, xn_ref = rest

    @pl.when(pl.program_id(1) == 0)
    def _():
        xn_ref[...] = _rms_scale(x_ref[...], g_ref[...]).astype(BF16)
        if side:
            os_ref[...] = _dot_f32(xn_ref[...], ws_ref[...])

    o_ref[...] = _dot_f32(xn_ref[...], w_ref[...]).astype(o_ref.dtype)


def norm_matmul(x, g, w, tm, tn, out_dtype=F32, w_side=None):
    t, d = x.shape
    n = w.shape[1]
    in_specs = [pl.BlockSpec((tm, d), lambda i, j: (i, 0)),
                pl.BlockSpec((1, d), lambda i, j: (0, 0)),
                pl.BlockSpec((d, tn), lambda i, j: (0, j))]
    out_specs = pl.BlockSpec((tm, tn), lambda i, j: (i, j))
    out_shape = jax.ShapeDtypeStruct((t, n), out_dtype)
    args = [x, g.reshape(1, d), w]
    if w_side is not None:
        in_specs.append(pl.BlockSpec((d, LANES), lambda i, j: (0, 0)))
        out_specs = (out_specs, pl.BlockSpec((tm, LANES), lambda i, j: (i, 0)))
        out_shape = (out_shape, jax.ShapeDtypeStruct((t, LANES), F32))
        args.append(w_side)
    return pl.pallas_call(
        _norm_matmul_body,
        grid=(t // tm, n // tn),
        in_specs=in_specs,
        out_specs=out_specs,
        out_shape=out_shape,
        scratch_shapes=[pltpu.VMEM((tm, d), BF16)],
        compiler_params=_params("arbitrary", "arbitrary"),
        name="norm_matmul",
    )(*args)


def _matmul_res_body(a_ref, w_ref, r_ref, o_ref):
    o_ref[...] = r_ref[...] + _dot_f32(a_ref[...].astype(BF16), w_ref[...])


def matmul_res(a, w, res, tm):
    t, k = a.shape
    n = w.shape[1]
    return pl.pallas_call(
        _matmul_res_body,
        grid=(t // tm,),
        in_specs=[pl.BlockSpec((tm, k), lambda i: (i, 0)),
                  pl.BlockSpec((k, n), lambda i: (0, 0)),
                  pl.BlockSpec((tm, n), lambda i: (i, 0))],
        out_specs=pl.BlockSpec((tm, n), lambda i: (i, 0)),
        out_shape=jax.ShapeDtypeStruct((t, n), F32),
        compiler_params=_params("arbitrary"),
        name="matmul_res",
    )(a, w, res)


def _mlp_body(x_ref, g_ref, wu_ref, wd_ref, o_ref, xn_ref):
    @pl.when(pl.program_id(1) == 0)
    def _():
        x = x_ref[...]
        xn_ref[...] = _rms_scale(x, g_ref[...]).astype(BF16)
        o_ref[...] = x

    h = jnp.maximum(_dot_f32(xn_ref[...], wu_ref[...]), 0.0)
    o_ref[...] += _dot_f32((h * h).astype(BF16), wd_ref[...])


def mlp(x, g, wu, wd, tm, tf):
    t, d = x.shape
    f = wu.shape[1]
    return pl.pallas_call(
        _mlp_body,
        grid=(t // tm, f // tf),
        in_specs=[pl.BlockSpec((tm, d), lambda i, j: (i, 0)),
                  pl.BlockSpec((1, d), lambda i, j: (0, 0)),
                  pl.BlockSpec((d, tf), lambda i, j: (0, j)),
                  pl.BlockSpec((tf, d), lambda i, j: (j, 0))],
        out_specs=pl.BlockSpec((tm, d), lambda i, j: (i, 0)),
        out_shape=jax.ShapeDtypeStruct((t, d), F32),
        scratch_shapes=[pltpu.VMEM((tm, d), BF16)],
        compiler_params=_params("arbitrary", "arbitrary"),
        name="mlp",
    )(x, g.reshape(1, d), wu, wd)


def _ssd_body(*refs, valid, zero_init, n_chunks, n_heads):
    if zero_init:
        (z_ref, x_ref, b_ref, c_ref, dt_ref, cw_ref, cb_ref, dtb_ref, alog_ref, dsk_ref, nw_ref,
         hx_ref, sh_ref, y_ref, cso_ref, sso_ref, tail_ref, act_ref, ht_ref) = refs
        cs_ref = ss_ref = None
    else:
        (z_ref, x_ref, b_ref, c_ref, dt_ref, cw_ref, cb_ref, dtb_ref, alog_ref, dsk_ref, nw_ref,
         hx_ref, sh_ref, cs_ref, ss_ref, y_ref, cso_ref, sso_ref, tail_ref, act_ref, ht_ref) = refs
    L = SSD_CHUNK
    G = SSD_GROUPS
    R = n_heads // G
    P = SSD_HEADDIM
    GP = R * P
    d_inner = n_heads * P
    gn = G * SSD_STATE
    conv_dim = d_inner + 2 * gn
    c = pl.program_id(1)

    @pl.when(c == 0)
    def _init():
        tail_ref[...] = jnp.zeros(tail_ref.shape, F32)
        if zero_init:
            ht_ref[...] = jnp.zeros(ht_ref.shape, F32)
        else:
            tail_ref[SUBLANES - (SSD_CONV - 1):SUBLANES, :] = cs_ref[0]
            for g in range(G):
                ht_ref[g] = ss_ref[0, R * g:R * (g + 1)].reshape(GP, SSD_STATE).T

    col_chunk = 512
    for j in range(conv_dim // col_chunk):
        cols = slice(j * col_chunk, (j + 1) * col_chunk)
        if cols.stop <= d_inner:
            xb = x_ref[0, :, cols]
        elif cols.stop <= d_inner + gn:
            xb = b_ref[0, :, cols.start - d_inner:cols.stop - d_inner]
        else:
            xb = c_ref[0, :, cols.start - d_inner - gn:cols.stop - d_inner - gn]
        xf = xb.astype(F32)
        edge = jnp.concatenate([tail_ref[:, cols], xf[0:SUBLANES]], axis=0)
        acc = cb_ref[:, cols] + cw_ref[SSD_CONV - 1:SSD_CONV, cols] * xf
        for k in range(1, SSD_CONV):
            delayed = jnp.concatenate([pltpu.roll(edge, k, 0)[SUBLANES:],
                                       _dot_f32(sh_ref[k - 1], xb)[SUBLANES:]], axis=0)
            acc = acc + cw_ref[SSD_CONV - 1 - k:SSD_CONV - k, cols] * delayed
        act_ref[:, cols] = _silu(acc)
        tail_ref[:, cols] = xf[L - SUBLANES:]
        cso_ref[0, :, cols] = xf[valid - (SSD_CONV - 1):valid]

    dt = _softplus(dt_ref[0] + dtb_ref[...])
    t_io = lax.broadcasted_iota(jnp.int32, (L, L), 0)
    s_io = lax.broadcasted_iota(jnp.int32, (L, L), 1)
    causal = t_io >= s_io
    if valid < L:
        row = lax.broadcasted_iota(jnp.int32, (L, LANES), 0)
        dt = jnp.where(row < valid, dt, 0.0)
    adt = dt * (-jnp.exp(alog_ref[...]))
    tri = jnp.where(causal, 1.0, 0.0).astype(BF16)
    acs = sum(_dot_f32(tri, p) for p in _split3(adt))
    hx = hx_ref[...]
    e_x = jnp.exp(sum(_dot_f32(p, hx) for p in _split3(acs)))
    alast = acs[L - 1:L, :]
    acs_t = acs.T
    src_t = (acs - jnp.log(dt)).T
    u_t = (dt * jnp.exp(alast - acs)).T
    head_of_lane = (lax.broadcasted_iota(jnp.int32, (L, GP), 1) // P).astype(F32).astype(BF16)

    for g in range(G):
        gc = slice(g * GP, (g + 1) * GP)
        bg = act_ref[:, d_inner + g * SSD_STATE:d_inner + (g + 1) * SSD_STATE]
        cg = act_ref[:, d_inner + gn + g * SSD_STATE:d_inner + gn + (g + 1) * SSD_STATE]
        cbf = cg.astype(BF16)
        cb = lax.dot_general(cbf, bg.astype(BF16), (((1,), (1,)), ((), ())),
                             preferred_element_type=F32)
        bg_t = bg.T
        h_t = ht_ref[g]
        y_off = _dot_f32(cbf, h_t.astype(BF16)) * e_x[:, gc]
        xg = act_ref[:, gc]
        xb = xg.astype(BF16)
        x_bd = jnp.concatenate([jnp.where(head_of_lane == r, xb, jnp.zeros_like(xb))
                                for r in range(R)], axis=0)
        ms, bs = [], []
        for r in range(R):
            h = g * R + r
            seg = acs[:, h:h + 1] - src_t[h:h + 1, :]
            ms.append((jnp.where(causal, jnp.exp(seg), 0.0) * cb).astype(BF16))
            bs.append((bg_t * u_t[h:h + 1, :]).astype(BF16))
        y_d = _dot_f32(jnp.concatenate(ms, axis=1), x_bd)
        upd = _dot_f32(jnp.concatenate(bs, axis=1), x_bd)
        yg = (y_d + y_off + dsk_ref[:, gc] * xg) * _silu(z_ref[0, :, gc].astype(F32))
        y_ref[0, :, gc] = _rms_scale(yg, nw_ref[:, gc]).astype(y_ref.dtype)
        ht_ref[g] = h_t * e_x[L - 1:L, gc] + upd

    @pl.when(c == n_chunks - 1)
    def _state_out():
        for g in range(G):
            sso_ref[0, R * g:R * (g + 1)] = ht_ref[g].T.reshape(R, P, SSD_STATE)


def ssd_scan(zx, dt_raw, conv_state, ssm_state, conv_w, conv_b, dt_bias, a_log, d_skip, norm_w, *, valid):
    b, s, _ = zx.shape
    n_heads = dt_bias.shape[0]
    d_inner = n_heads * SSD_HEADDIM
    gn = SSD_GROUPS * SSD_STATE
    conv_dim = d_inner + 2 * gn
    L = SSD_CHUNK
    n_chunks = s // L
    zero_init = conv_state is None
    pad_h = LANES - n_heads
    head_expand = (jnp.arange(LANES)[:, None] == jnp.arange(d_inner)[None, :] // SSD_HEADDIM)
    consts = [conv_w, conv_b.reshape(1, conv_dim),
              jnp.pad(dt_bias, (0, pad_h)).reshape(1, LANES),
              jnp.pad(a_log, (0, pad_h)).reshape(1, LANES),
              jnp.repeat(d_skip, SSD_HEADDIM).reshape(1, d_inner),
              norm_w.reshape(1, d_inner),
              head_expand.astype(BF16)]
    const_specs = [pl.BlockSpec(a.shape, lambda i, j: (0, 0)) for a in consts]
    t_io = jnp.arange(L)
    shifts = jnp.stack([t_io[:, None] - k == t_io[None, :] for k in range(1, SSD_CONV)]).astype(BF16)
    consts.append(shifts)
    const_specs.append(pl.BlockSpec(shifts.shape, lambda i, j: (0, 0, 0)))
    in_specs = [pl.BlockSpec((1, L, d_inner), lambda i, j: (i, j, 0)),
                pl.BlockSpec((1, L, d_inner), lambda i, j: (i, j, 1)),
                pl.BlockSpec((1, L, gn), lambda i, j: (i, j, 2 * d_inner // gn)),
                pl.BlockSpec((1, L, gn), lambda i, j: (i, j, 2 * d_inner // gn + 1)),
                pl.BlockSpec((1, L, LANES), lambda i, j: (i, j, 0)),
                ] + const_specs
    args = [zx, zx, zx, zx, dt_raw] + consts
    if not zero_init:
        in_specs += [pl.BlockSpec((1, SSD_CONV - 1, conv_dim), lambda i, j: (i, 0, 0)),
                     pl.BlockSpec((1, n_heads, SSD_HEADDIM, SSD_STATE), lambda i, j: (i, 0, 0, 0))]
        args += [conv_state, ssm_state]
    out_shape = (jax.ShapeDtypeStruct((b, s, d_inner), BF16),
                 jax.ShapeDtypeStruct((b, SSD_CONV - 1, conv_dim), F32),
                 jax.ShapeDtypeStruct((b, n_heads, SSD_HEADDIM, SSD_STATE), F32))
    out_specs = (pl.BlockSpec((1, L, d_inner), lambda i, j: (i, j, 0)),
                 pl.BlockSpec((1, SSD_CONV - 1, conv_dim), lambda i, j: (i, 0, 0)),
                 pl.BlockSpec((1, n_heads, SSD_HEADDIM, SSD_STATE), lambda i, j: (i, 0, 0, 0)))
    body = functools.partial(_ssd_body, valid=valid, zero_init=zero_init, n_chunks=n_chunks,
                             n_heads=n_heads)
    return pl.pallas_call(
        body,
        grid=(b, n_chunks),
        in_specs=in_specs,
        out_specs=out_specs,
        out_shape=out_shape,
        scratch_shapes=[pltpu.VMEM((SUBLANES, conv_dim), F32),
                        pltpu.VMEM((L, conv_dim), F32),
                        pltpu.VMEM((SSD_GROUPS, SSD_STATE, d_inner // SSD_GROUPS), F32)],
        compiler_params=_params("arbitrary", "arbitrary"),
        name="ssd_scan",
    )(*args)


def _qk_norm_body(q_ref, k_ref, v_ref, gq_ref, gk_ref, avg_ref, qo_ref, ko_ref, vo_ref, *, scale):
    def head_norm(x, g):
        slab = avg_ref.shape[0]
        sq = (x * x).astype(BF16)
        ms = jnp.concatenate([_dot_f32(sq[:, c:c + slab], avg_ref[...])
                              for c in range(0, x.shape[1], slab)], axis=1)
        return x * lax.rsqrt(ms + EPS) * g

    qo_ref[...] = (head_norm(q_ref[...], gq_ref[...]) * scale).astype(qo_ref.dtype)
    ko_ref[...] = head_norm(k_ref[...], gk_ref[...])
    vo_ref[...] = v_ref[...]


def qk_norm(qkv, q_gain, k_gain, tm):
    t, n3 = qkv.shape
    d = n3 // 3
    n_heads = d // SB_HEADDIM
    head_of = jnp.arange(MXU_WIDTH) // SB_HEADDIM
    avg = jnp.where(head_of[:, None] == head_of[None, :], 1.0 / SB_HEADDIM, 0.0).astype(BF16)
    gq = jnp.tile(q_gain, n_heads).reshape(1, d)
    gk = jnp.tile(k_gain, n_heads).reshape(1, d)
    body = functools.partial(_qk_norm_body, scale=0.5 * SB_HEADDIM ** -0.5)
    row_spec = pl.BlockSpec((tm, d), lambda i: (i, 0))
    return pl.pallas_call(
        body,
        grid=(t // tm,),
        in_specs=[pl.BlockSpec((tm, d), lambda i: (i, 0)),
                  pl.BlockSpec((tm, d), lambda i: (i, 1)),
                  pl.BlockSpec((tm, d), lambda i: (i, 2)),
                  pl.BlockSpec((1, d), lambda i: (0, 0)),
                  pl.BlockSpec((1, d), lambda i: (0, 0)),
                  pl.BlockSpec((MXU_WIDTH, MXU_WIDTH), lambda i: (0, 0))],
        out_specs=(row_spec, row_spec, row_spec),
        out_shape=(jax.ShapeDtypeStruct((t, d), BF16), jax.ShapeDtypeStruct((t, d), F32),
                   jax.ShapeDtypeStruct((t, d), F32)),
        compiler_params=_params("arbitrary"),
        name="qk_norm",
    )(qkv, qkv, qkv, gq, gk, avg)


def _kv_layout_body(k_ref, v_ref, perm_ref, kp_ref, vt_ref, *, blocks):
    pairs = kp_ref.shape[1]
    perm = perm_ref[...]
    for blk in range(blocks):
        rows = slice(blk * KEY_BLOCK, (blk + 1) * KEY_BLOCK)
        kp = _dot_f32(perm, k_ref[0, rows, :].astype(BF16))
        vp = _dot_f32(perm, v_ref[0, rows, :].astype(BF16))
        for p in range(pairs):
            lanes = slice(p * LANES, (p + 1) * LANES)
            kp_ref[0, p, blk] = kp[:, lanes].astype(BF16)
            vt_ref[0, p, blk] = vp[:, lanes].T.astype(BF16)


def kv_layout(k, v, v_col):
    b, s, d = k.shape
    nkb = s // KEY_BLOCK
    blocks = 8 if nkb % 8 == 0 else nkb
    pairs = d // LANES
    r = jnp.arange(KEY_BLOCK)
    key_of_row = (r % SUBLANES) * VREGS_PER_KEY_BLOCK + r // SUBLANES
    perm = (key_of_row[:, None] == r[None, :]).astype(BF16)
    out = jax.ShapeDtypeStruct((b, pairs, nkb, KEY_BLOCK, LANES), BF16)
    out_spec = pl.BlockSpec((1, pairs, blocks, KEY_BLOCK, LANES), lambda i, j: (i, 0, j, 0, 0))
    return pl.pallas_call(
        functools.partial(_kv_layout_body, blocks=blocks),
        grid=(b, nkb // blocks),
        in_specs=[pl.BlockSpec((1, blocks * KEY_BLOCK, d), lambda i, j: (i, j, 0)),
                  pl.BlockSpec((1, blocks * KEY_BLOCK, d), lambda i, j: (i, j, v_col)),
                  pl.BlockSpec((KEY_BLOCK, KEY_BLOCK), lambda i, j: (0, 0))],
        out_specs=(out_spec, out_spec),
        out_shape=(out, out),
        compiler_params=_params("arbitrary", "arbitrary"),
        name="kv_layout",
    )(k, v, perm)


def _attn_body(q_ref, k_ref, vt_ref, o_ref, zbuf_ref, wbuf_ref, *, tq, q_pos0):
    nv = VREGS_PER_KEY_BLOCK
    qi = pl.program_id(2)
    q = q_ref[0]
    lane = lax.broadcasted_iota(jnp.int32, (tq, LANES), 1)
    qms = [jnp.where(lane // SB_HEADDIM == hh, q, jnp.zeros_like(q)) for hh in range(2)]
    r_io = lax.broadcasted_iota(jnp.int32, (KEY_BLOCK, KEY_BLOCK), 0)
    diag_mask = ((r_io % SUBLANES) * nv + r_io // SUBLANES
                 < lax.broadcasted_iota(jnp.int32, (KEY_BLOCK, KEY_BLOCK), 1))
    n_diag = tq // KEY_BLOCK
    n_clear = (q_pos0 + qi * tq) // KEY_BLOCK

    def shift_up(a, k):
        sub = lax.broadcasted_iota(jnp.int32, a.shape, 0)
        return jnp.where(sub < SUBLANES - k, pltpu.roll(a, SUBLANES - k, 0), 1.0)

    def scores(j0, n, hh):
        kb = k_ref[0, 0, pl.ds(j0, n)].reshape(n * KEY_BLOCK, LANES)
        return lax.dot_general(kb, qms[hh], (((1,), (1,)), ((), ())), preferred_element_type=F32)

    def values(j0, n, hh, w):
        last = k_ref.shape[2] - 1
        vt = jnp.concatenate([vt_ref[0, 0, jnp.minimum(j0 + r, last)][hh * SB_HEADDIM:(hh + 1) * SB_HEADDIM, :]
                              for r in range(n)], axis=1)
        return _dot_f32(vt, w)

    def weights(z, n, rest, diag):
        ws = [None] * n
        for r in range(n - 1, -1, -1):
            lo = r * KEY_BLOCK if diag else 0
            width = tq - lo
            om = 0.5 - 0.5 * jnp.tanh(z[r * KEY_BLOCK:(r + 1) * KEY_BLOCK, lo:])
            if diag:
                om_d = jnp.where(diag_mask, om[:, :KEY_BLOCK], 1.0)
                om = jnp.concatenate([om_d, om[:, KEY_BLOCK:]], axis=1) if width > KEY_BLOCK else om_d
            om = om.reshape(nv, SUBLANES, width)
            suffix = [None] * (nv + 1)
            suffix[nv] = 1.0
            suffix[nv - 1] = om[nv - 1]
            for v in range(nv - 2, -1, -1):
                suffix[v] = om[v] * suffix[v + 1]
            row_tot = suffix[0]
            later = shift_up(row_tot, 1)
            later = later * shift_up(later, 1)
            later = later * shift_up(later, 2)
            later = later * shift_up(later, 4)
            rest_r = rest[:, lo:]
            f = later * rest_r
            w = jnp.stack([(suffix[v + 1] - suffix[v]) * f for v in range(nv)],
                          axis=0).reshape(KEY_BLOCK, width).astype(BF16)
            tot = jnp.broadcast_to((row_tot * later)[0:1, :], (SUBLANES, width))
            if lo:
                w = jnp.concatenate([jnp.zeros((KEY_BLOCK, lo), BF16), w], axis=1)
                rest = jnp.concatenate([rest[:, :lo], rest_r * tot], axis=1)
            else:
                rest = rest_r * tot
            ws[r] = w
        return jnp.concatenate(ws, axis=0), rest

    nb = KEY_BLOCKS_PER_STEP
    n_far = n_clear // nb

    def group_start(t):
        return n_clear - nb * t

    ones = jnp.ones((SUBLANES, tq), F32)
    z0 = scores(n_clear, n_diag, 0)
    z1 = scores(n_clear, n_diag, 1)
    zbuf_ref[...] = scores(jnp.maximum(group_start(1), 0), nb, 0)
    w0, rest0 = weights(z0, n_diag, ones, True)
    acc0 = values(n_clear, n_diag, 0, w0)
    w1, rest1 = weights(z1, n_diag, ones, True)
    acc1 = jnp.zeros((SB_HEADDIM, tq), F32)
    if n_diag > nb:
        acc1 = values(n_clear, n_diag, 1, w1)
        wbuf_ref[...] = jnp.zeros(wbuf_ref.shape, BF16)
    else:
        if n_diag < nb:
            wbuf_ref[n_diag * KEY_BLOCK:, :] = jnp.zeros(((nb - n_diag) * KEY_BLOCK, tq), BF16)
        wbuf_ref[0:n_diag * KEY_BLOCK, :] = w1

    def far(t, carry):
        rest0, acc0, rest1, acc1 = carry
        j0 = group_start(t)
        acc1 = acc1 + values(j0 + nb, nb, 1, wbuf_ref[...])
        z1 = scores(j0, nb, 1)
        w0, rest0 = weights(zbuf_ref[...], nb, rest0, False)
        acc0 = acc0 + values(j0, nb, 0, w0)
        w1, rest1 = weights(z1, nb, rest1, False)
        wbuf_ref[...] = w1
        zbuf_ref[...] = scores(jnp.maximum(j0 - nb, 0), nb, 0)
        return rest0, acc0, rest1, acc1

    _, acc0, _, acc1 = lax.fori_loop(1, n_far + 1, far, (rest0, acc0, rest1, acc1))
    acc1 = acc1 + values(group_start(n_far), nb, 1, wbuf_ref[...])
    o_ref[0] = jnp.concatenate([acc0, acc1], axis=0).T.astype(o_ref.dtype)


def sb_attention(q, kp, vt, *, tq, q_pos0):
    b, sq, d = q.shape
    pairs, nkb = kp.shape[1], kp.shape[2]
    n_tiles = sq // tq
    for t in range(n_tiles):
        assert (q_pos0 + t * tq) % (KEY_BLOCK * KEY_BLOCKS_PER_STEP) == 0
    assert q_pos0 + sq <= nkb * KEY_BLOCK
    body = functools.partial(_attn_body, tq=tq, q_pos0=q_pos0)
    return pl.pallas_call(
        body,
        grid=(b, pairs, n_tiles),
        in_specs=[pl.BlockSpec((1, tq, LANES), lambda i, p, t: (i, t, p)),
                  pl.BlockSpec((1, 1, nkb, KEY_BLOCK, LANES), lambda i, p, t: (i, p, 0, 0, 0)),
                  pl.BlockSpec((1, 1, nkb, LANES, KEY_BLOCK), lambda i, p, t: (i, p, 0, 0, 0))],
        out_specs=pl.BlockSpec((1, tq, LANES), lambda i, p, t: (i, t, p)),
        out_shape=jax.ShapeDtypeStruct((b, sq, d), BF16),
        scratch_shapes=[pltpu.VMEM((KEY_BLOCKS_PER_STEP * KEY_BLOCK, tq), F32),
                        pltpu.VMEM((KEY_BLOCKS_PER_STEP * KEY_BLOCK, tq), BF16)],
        compiler_params=_params("arbitrary", "arbitrary", "arbitrary"),
        name="sb_attention",
    )(q, kp, vt)


def _row_tile(t):
    return 1024 if t % 1024 == 0 else t


def _ssd_layer(y, b, s, conv_state, ssm_state, g_mix, in_w, conv_w, conv_b, dt_bias, a_log, d_skip,
               norm_w, out_w):
    t, d = y.shape
    n_heads = dt_bias.shape[0]
    n_main = in_w.shape[1] - n_heads
    w_dt = jnp.pad(in_w[:, n_main:], ((0, 0), (0, LANES - n_heads))).astype(BF16)
    zx, dt_raw = norm_matmul(y, g_mix, in_w[:, :n_main].astype(BF16), _row_tile(t), 1024,
                             out_dtype=BF16, w_side=w_dt)
    zx = zx.reshape(b, s, n_main)
    dt_raw = dt_raw.reshape(b, s, LANES)
    valid = SSD_CHUNK
    if s % SSD_CHUNK:
        assert s < SSD_CHUNK
        valid = s
        zx = jnp.pad(zx, ((0, 0), (0, SSD_CHUNK - s), (0, 0)))
        dt_raw = jnp.pad(dt_raw, ((0, 0), (0, SSD_CHUNK - s), (0, 0)))
    y_ssd, conv_out, ssm_out = ssd_scan(zx, dt_raw, conv_state, ssm_state, conv_w, conv_b, dt_bias,
                                        a_log, d_skip, norm_w, valid=valid)
    y_ssd = y_ssd[:, :s].reshape(t, -1)
    return matmul_res(y_ssd, out_w.astype(BF16), y, _row_tile(t)), conv_out, ssm_out


def _sb_layer(y, b, s, cache_k, cache_v, g_mix, qkv_w, q_gain, k_gain, out_w):
    t, d = y.shape
    tm = _row_tile(t)
    qkv = norm_matmul(y, g_mix, qkv_w.astype(BF16), tm, 1024)
    q, k, v = qk_norm(qkv, q_gain, k_gain, tm)
    q = q.reshape(b, s, d)
    if cache_k is None:
        tq, q_pos0 = KEY_BLOCKS_PER_STEP * KEY_BLOCK, 0
        kp, vt = kv_layout(k.reshape(b, s, d), qkv.reshape(b, s, 3 * d), 2)
    else:
        past = cache_k.shape[1]
        tq, q_pos0 = KEY_BLOCK, past
        pad = ((0, 0), (0, tq - s), (0, 0))
        kp_c, vt_c = kv_layout(cache_k.reshape(b, past, d), cache_v.reshape(b, past, d), 0)
        kp_n, vt_n = kv_layout(jnp.pad(k.reshape(b, s, d), pad), jnp.pad(v.reshape(b, s, d), pad), 0)
        kp = jnp.concatenate([kp_c, kp_n], axis=2)
        vt = jnp.concatenate([vt_c, vt_n], axis=2)
        q = jnp.pad(q, pad)
    o = sb_attention(q, kp, vt, tq=tq, q_pos0=q_pos0)[:, :s].reshape(t, d)
    heads = d // SB_HEADDIM
    return (matmul_res(o, out_w.astype(BF16), y, tm),
            k.reshape(b, s, heads, SB_HEADDIM), v.reshape(b, s, heads, SB_HEADDIM))


def kernel(x_prompt, x_sample, state_ssm, state_conv, cache_k, cache_v, norm_mix, norm_mlp, ssd_in_w, ssd_conv_w, ssd_conv_b, ssd_dt_bias, ssd_a_log, ssd_d, ssd_norm_w, ssd_out_w, sb_qkv_w, sb_q_gain, sb_k_gain, sb_out_w, mlp_up, mlp_down):
    bp, sp, d = x_prompt.shape
    bs, ss, _ = x_sample.shape
    depth = norm_mix.shape[0]
    y_p = x_prompt.reshape(bp * sp, d)
    y_s = x_sample.reshape(bs * ss, d)
    p_ssm, p_conv, p_k, p_v = [], [], [], []
    s_ssm, s_conv, s_k, s_v = [], [], [], []
    for i in range(depth):
        j = i // 2
        if i % 2 == 0:
            prm = (norm_mix[i], ssd_in_w[j], ssd_conv_w[j], ssd_conv_b[j], ssd_dt_bias[j],
                   ssd_a_log[j], ssd_d[j], ssd_norm_w[j], ssd_out_w[j])
            y_p, c_p, st_p = _ssd_layer(y_p, bp, sp, None, None, *prm)
            y_s, c_s, st_s = _ssd_layer(y_s, bs, ss, state_conv[j], state_ssm[j], *prm)
            p_ssm.append(st_p); p_conv.append(c_p)
            s_ssm.append(st_s); s_conv.append(c_s)
        else:
            prm = (norm_mix[i], sb_qkv_w[j], sb_q_gain[j], sb_k_gain[j], sb_out_w[j])
            y_p, k_p, v_p = _sb_layer(y_p, bp, sp, None, None, *prm)
            y_s, k_s, v_s = _sb_layer(y_s, bs, ss, cache_k[j], cache_v[j], *prm)
            p_k.append(k_p); p_v.append(v_p)
            s_k.append(k_s); s_v.append(v_s)
        wu = mlp_up[i].astype(BF16)
        wd = mlp_down[i].astype(BF16)
        y_p = mlp(y_p, norm_mlp[i], wu, wd, _row_tile(y_p.shape[0]), 1024)
        y_s = mlp(y_s, norm_mlp[i], wu, wd, _row_tile(y_s.shape[0]), 1024)
    return (y_p.reshape(bp, sp, d), y_s.reshape(bs, ss, d),
            jnp.stack(p_ssm), jnp.stack(p_conv), jnp.stack(p_k), jnp.stack(p_v),
            jnp.stack(s_ssm), jnp.stack(s_conv), jnp.stack(s_k), jnp.stack(s_v))
```

```python
import functools

import jax
import jax.numpy as jnp
from jax import lax
from jax.experimental import pallas as pl
from jax.experimental.pallas import tpu as pltpu

F32 = jnp.float32
BF16 = jnp.bfloat16
EPS = 1e-6

LANES = 128
SUBLANES = 8
MXU_WIDTH = 256
VMEM_LIMIT_BYTES = 56 * 1024 * 1024

SSD_HEADDIM = 64
SSD_GROUPS = 8
SSD_STATE = 128
SSD_CONV = 4
SB_HEADDIM = 64
SSD_CHUNK = 128
SSD_CHUNKS_PER_STEP = 2
KEY_BLOCK = 128
VREGS_PER_KEY_BLOCK = KEY_BLOCK // SUBLANES
KEY_BLOCKS_PER_STEP = 4


def _params(*sem):
    return pltpu.CompilerParams(dimension_semantics=sem, vmem_limit_bytes=VMEM_LIMIT_BYTES)


def _rms_scale(x, g):
    ms = jnp.mean(x * x, axis=-1, keepdims=True)
    return x * lax.rsqrt(ms + EPS) * g


def _silu(x):
    h = 0.5 * x
    return h + h * jnp.tanh(h)


def _softplus(x):
    return jnp.maximum(x, 0.0) + jnp.log(1.0 + jnp.exp(-jnp.abs(x)))


def _split3(a):
    hi = a.astype(BF16)
    rem = a - hi.astype(F32)
    mid = rem.astype(BF16)
    lo = (rem - mid.astype(F32)).astype(BF16)
    return hi, mid, lo


def _dot_f32(a, b):
    return jnp.dot(a, b, preferred_element_type=F32)


def _norm_matmul_body(x_ref, g_ref, w_ref, *rest):
    side = len(rest) == 4
    if side:
        ws_ref, o_ref, os_ref, xn_ref = rest
    else:
        o_ref, xn_ref = rest

    @pl.when(pl.program_id(1) == 0)
    def _():
        xn_ref[...] = _rms_scale(x_ref[...], g_ref[...]).astype(BF16)
        if side:
            os_ref[...] = _dot_f32(xn_ref[...], ws_ref[...])

    o_ref[...] = _dot_f32(xn_ref[...], w_ref[...]).astype(o_ref.dtype)


def norm_matmul(x, g, w, tm, tn, out_dtype=F32, w_side=None):
    t, d = x.shape
    n = w.shape[1]
    in_specs = [pl.BlockSpec((tm, d), lambda i, j: (i, 0)),
                pl.BlockSpec((1, d), lambda i, j: (0, 0)),
                pl.BlockSpec((d, tn), lambda i, j: (0, j))]
    out_specs = pl.BlockSpec((tm, tn), lambda i, j: (i, j))
    out_shape = jax.ShapeDtypeStruct((t, n), out_dtype)
    args = [x, g.reshape(1, d), w]
    if w_side is not None:
        in_specs.append(pl.BlockSpec((d, LANES), lambda i, j: (0, 0)))
        out_specs = (out_specs, pl.BlockSpec((tm, LANES), lambda i, j: (i, 0)))
        out_shape = (out_shape, jax.ShapeDtypeStruct((t, LANES), F32))
        args.append(w_side)
    return pl.pallas_call(
        _norm_matmul_body,
        grid=(t // tm, n // tn),
        in_specs=in_specs,
        out_specs=out_specs,
        out_shape=out_shape,
        scratch_shapes=[pltpu.VMEM((tm, d), BF16)],
        compiler_params=_params("arbitrary", "arbitrary"),
        name="norm_matmul",
    )(*args)


def _matmul_res_body(a_ref, w_ref, r_ref, o_ref):
    o_ref[...] = r_ref[...] + _dot_f32(a_ref[...].astype(BF16), w_ref[...])


def matmul_res(a, w, res, tm):
    t, k = a.shape
    n = w.shape[1]
    return pl.pallas_call(
        _matmul_res_body,
        grid=(t // tm,),
        in_specs=[pl.BlockSpec((tm, k), lambda i: (i, 0)),
                  pl.BlockSpec((k, n), lambda i: (0, 0)),
                  pl.BlockSpec((tm, n), lambda i: (i, 0))],
        out_specs=pl.BlockSpec((tm, n), lambda i: (i, 0)),
        out_shape=jax.ShapeDtypeStruct((t, n), F32),
        compiler_params=_params("arbitrary"),
        name="matmul_res",
    )(a, w, res)


def _mlp_body(x_ref, g_ref, wu_ref, wd_ref, o_ref, xn_ref):
    @pl.when(pl.program_id(1) == 0)
    def _():
        x = x_ref[...]
        xn_ref[...] = _rms_scale(x, g_ref[...]).astype(BF16)
        o_ref[...] = x

    h = jnp.maximum(_dot_f32(xn_ref[...], wu_ref[...]), 0.0)
    o_ref[...] += _dot_f32((h * h).astype(BF16), wd_ref[...])


def mlp(x, g, wu, wd, tm, tf):
    t, d = x.shape
    f = wu.shape[1]
    return pl.pallas_call(
        _mlp_body,
        grid=(t // tm, f // tf),
        in_specs=[pl.BlockSpec((tm, d), lambda i, j: (i, 0)),
                  pl.BlockSpec((1, d), lambda i, j: (0, 0)),
                  pl.BlockSpec((d, tf), lambda i, j: (0, j)),
                  pl.BlockSpec((tf, d), lambda i, j: (j, 0))],
        out_specs=pl.BlockSpec((tm, d), lambda i, j: (i, 0)),
        out_shape=jax.ShapeDtypeStruct((t, d), F32),
        scratch_shapes=[pltpu.VMEM((tm, d), BF16)],
        compiler_params=_params("arbitrary", "arbitrary"),
        name="mlp",
    )(x, g.reshape(1, d), wu, wd)


def _ssd_body(*refs, chunks, **static):
    for cc in range(chunks):
        _ssd_chunk(*refs, row0=cc * SSD_CHUNK, last=cc == chunks - 1, **static)


def _ssd_chunk(*refs, row0, last, valid, zero_init, n_steps, n_heads):
    if zero_init:
        (z_ref, x_ref, b_ref, c_ref, dt_ref, cw_ref, cb_ref, dtb_ref, alog_ref, dsk_ref, nw_ref,
         hx_ref, sh_ref, y_ref, cso_ref, sso_ref, tail_ref, act_ref, ht_ref) = refs
        cs_ref = ss_ref = None
    else:
        (z_ref, x_ref, b_ref, c_ref, dt_ref, cw_ref, cb_ref, dtb_ref, alog_ref, dsk_ref, nw_ref,
         hx_ref, sh_ref, cs_ref, ss_ref, y_ref, cso_ref, sso_ref, tail_ref, act_ref, ht_ref) = refs
    L = SSD_CHUNK
    G = SSD_GROUPS
    R = n_heads // G
    P = SSD_HEADDIM
    GP = R * P
    d_inner = n_heads * P
    gn = G * SSD_STATE
    conv_dim = d_inner + 2 * gn
    rows = slice(row0, row0 + L)
    z_ref, x_ref, b_ref, c_ref, dt_ref, y_ref = (
        r.at[:, rows] for r in (z_ref, x_ref, b_ref, c_ref, dt_ref, y_ref))
    act_ref = act_ref.at[row0 // L]
    step = pl.program_id(1)

    @pl.when((step == 0) & (row0 == 0))
    def _init():
        tail_ref[...] = jnp.zeros(tail_ref.shape, F32)
        if zero_init:
            ht_ref[...] = jnp.zeros(ht_ref.shape, F32)
        else:
            tail_ref[SUBLANES - (SSD_CONV - 1):SUBLANES, :] = cs_ref[0]
            for g in range(G):
                ht_ref[g] = ss_ref[0, R * g:R * (g + 1)].reshape(GP, SSD_STATE).T

    col_chunk = 512
    for j in range(conv_dim // col_chunk):
        cols = slice(j * col_chunk, (j + 1) * col_chunk)
        if cols.stop <= d_inner:
            xb = x_ref[0, :, cols]
        elif cols.stop <= d_inner + gn:
            xb = b_ref[0, :, cols.start - d_inner:cols.stop - d_inner]
        else:
            xb = c_ref[0, :, cols.start - d_inner - gn:cols.stop - d_inner - gn]
        xf = xb.astype(F32)
        edge = jnp.concatenate([tail_ref[:, cols], xf[0:SUBLANES]], axis=0)
        acc = cb_ref[:, cols] + cw_ref[SSD_CONV - 1:SSD_CONV, cols] * xf
        for k in range(1, SSD_CONV):
            delayed = jnp.concatenate([pltpu.roll(edge, k, 0)[SUBLANES:],
                                       _dot_f32(sh_ref[k - 1], xb)[SUBLANES:]], axis=0)
            acc = acc + cw_ref[SSD_CONV - 1 - k:SSD_CONV - k, cols] * delayed
        act_ref[:, cols] = _silu(acc)
        tail_ref[:, cols] = xf[L - SUBLANES:]
        cso_ref[0, :, cols] = xf[valid - (SSD_CONV - 1):valid]

    dt = _softplus(dt_ref[0] + dtb_ref[...])
    t_io = lax.broadcasted_iota(jnp.int32, (L, L), 0)
    s_io = lax.broadcasted_iota(jnp.int32, (L, L), 1)
    causal = t_io >= s_io
    if valid < L:
        row = lax.broadcasted_iota(jnp.int32, (L, LANES), 0)
        dt = jnp.where(row < valid, dt, 0.0)
    adt = dt * (-jnp.exp(alog_ref[...]))
    tri = jnp.where(causal, 1.0, 0.0).astype(BF16)
    acs = sum(_dot_f32(tri, p) for p in _split3(adt))
    hx = hx_ref[...]
    e_x = jnp.exp(sum(_dot_f32(p, hx) for p in _split3(acs)))
    alast = acs[L - 1:L, :]
    acs_t = acs.T
    src_t = (acs - jnp.log(dt)).T
    u_t = (dt * jnp.exp(alast - acs)).T
    head_of_lane = (lax.broadcasted_iota(jnp.int32, (L, GP), 1) // P).astype(F32).astype(BF16)

    for g in range(G):
        gc = slice(g * GP, (g + 1) * GP)
        bg = act_ref[:, d_inner + g * SSD_STATE:d_inner + (g + 1) * SSD_STATE]
        cg = act_ref[:, d_inner + gn + g * SSD_STATE:d_inner + gn + (g + 1) * SSD_STATE]
        cbf = cg.astype(BF16)
        cb = lax.dot_general(cbf, bg.astype(BF16), (((1,), (1,)), ((), ())),
                             preferred_element_type=F32)
        bg_t = bg.T
        h_t = ht_ref[g]
        y_off = _dot_f32(cbf, h_t.astype(BF16)) * e_x[:, gc]
        xg = act_ref[:, gc]
        xb = xg.astype(BF16)
        x_bd = jnp.concatenate([jnp.where(head_of_lane == r, xb, jnp.zeros_like(xb))
                                for r in range(R)], axis=0)
        ms, bs = [], []
        for r in range(R):
            h = g * R + r
            seg = acs[:, h:h + 1] - src_t[h:h + 1, :]
            ms.append((jnp.where(causal, jnp.exp(seg), 0.0) * cb).astype(BF16))
            bs.append((bg_t * u_t[h:h + 1, :]).astype(BF16))
        y_d = _dot_f32(jnp.concatenate(ms, axis=1), x_bd)
        upd = _dot_f32(jnp.concatenate(bs, axis=1), x_bd)
        yg = (y_d + y_off + dsk_ref[:, gc] * xg) * _silu(z_ref[0, :, gc].astype(F32))
        y_ref[0, :, gc] = _rms_scale(yg, nw_ref[:, gc]).astype(y_ref.dtype)
        ht_ref[g] = h_t * e_x[L - 1:L, gc] + upd

    @pl.when((step == n_steps - 1) & last)
    def _state_out():
        for g in range(G):
            sso_ref[0, R * g:R * (g + 1)] = ht_ref[g].T.reshape(R, P, SSD_STATE)


def ssd_scan(zx, dt_raw, conv_state, ssm_state, conv_w, conv_b, dt_bias, a_log, d_skip, norm_w, *, valid):
    b, s, _ = zx.shape
    n_heads = dt_bias.shape[0]
    d_inner = n_heads * SSD_HEADDIM
    gn = SSD_GROUPS * SSD_STATE
    conv_dim = d_inner + 2 * gn
    L = SSD_CHUNK
    n_chunks = s // L
    chunks = SSD_CHUNKS_PER_STEP if n_chunks % SSD_CHUNKS_PER_STEP == 0 else 1
    rows = chunks * L
    zero_init = conv_state is None
    pad_h = LANES - n_heads
    head_expand = (jnp.arange(LANES)[:, None] == jnp.arange(d_inner)[None, :] // SSD_HEADDIM)
    consts = [conv_w, conv_b.reshape(1, conv_dim),
              jnp.pad(dt_bias, (0, pad_h)).reshape(1, LANES),
              jnp.pad(a_log, (0, pad_h)).reshape(1, LANES),
              jnp.repeat(d_skip, SSD_HEADDIM).reshape(1, d_inner),
              norm_w.reshape(1, d_inner),
              head_expand.astype(BF16)]
    const_specs = [pl.BlockSpec(a.shape, lambda i, j: (0, 0)) for a in consts]
    t_io = jnp.arange(L)
    shifts = jnp.stack([t_io[:, None] - k == t_io[None, :] for k in range(1, SSD_CONV)]).astype(BF16)
    consts.append(shifts)
    const_specs.append(pl.BlockSpec(shifts.shape, lambda i, j: (0, 0, 0)))
    in_specs = [pl.BlockSpec((1, rows, d_inner), lambda i, j: (i, j, 0)),
                pl.BlockSpec((1, rows, d_inner), lambda i, j: (i, j, 1)),
                pl.BlockSpec((1, rows, gn), lambda i, j: (i, j, 2 * d_inner // gn)),
                pl.BlockSpec((1, rows, gn), lambda i, j: (i, j, 2 * d_inner // gn + 1)),
                pl.BlockSpec((1, rows, LANES), lambda i, j: (i, j, 0)),
                ] + const_specs
    args = [zx, zx, zx, zx, dt_raw] + consts
    if not zero_init:
        in_specs += [pl.BlockSpec((1, SSD_CONV - 1, conv_dim), lambda i, j: (i, 0, 0)),
                     pl.BlockSpec((1, n_heads, SSD_HEADDIM, SSD_STATE), lambda i, j: (i, 0, 0, 0))]
        args += [conv_state, ssm_state]
    out_shape = (jax.ShapeDtypeStruct((b, s, d_inner), BF16),
                 jax.ShapeDtypeStruct((b, SSD_CONV - 1, conv_dim), F32),
                 jax.ShapeDtypeStruct((b, n_heads, SSD_HEADDIM, SSD_STATE), F32))
    out_specs = (pl.BlockSpec((1, rows, d_inner), lambda i, j: (i, j, 0)),
                 pl.BlockSpec((1, SSD_CONV - 1, conv_dim), lambda i, j: (i, 0, 0)),
                 pl.BlockSpec((1, n_heads, SSD_HEADDIM, SSD_STATE), lambda i, j: (i, 0, 0, 0)))
    body = functools.partial(_ssd_body, chunks=chunks, valid=valid, zero_init=zero_init,
                             n_steps=n_chunks // chunks, n_heads=n_heads)
    return pl.pallas_call(
        body,
        grid=(b, n_chunks // chunks),
        in_specs=in_specs,
        out_specs=out_specs,
        out_shape=out_shape,
        scratch_shapes=[pltpu.VMEM((SUBLANES, conv_dim), F32),
                        pltpu.VMEM((chunks, L, conv_dim), F32),
                        pltpu.VMEM((SSD_GROUPS, SSD_STATE, d_inner // SSD_GROUPS), F32)],
        compiler_params=_params("arbitrary", "arbitrary"),
        name="ssd_scan",
    )(*args)


def _qkv_body(x_ref, g_ref, w_ref, gq_ref, gk_ref, avg_ref, qo_ref, ko_ref, vo_ref, *, scale):
    d = x_ref.shape[1]
    xn = _rms_scale(x_ref[...], g_ref[...]).astype(BF16)

    def head_norm(x, g):
        slab = avg_ref.shape[0]
        sq = (x * x).astype(BF16)
        ms = jnp.concatenate([_dot_f32(sq[:, c:c + slab], avg_ref[...])
                              for c in range(0, x.shape[1], slab)], axis=1)
        return x * lax.rsqrt(ms + EPS) * g

    qo_ref[...] = (head_norm(_dot_f32(xn, w_ref[:, 0:d]), gq_ref[...]) * scale).astype(qo_ref.dtype)
    ko_ref[...] = head_norm(_dot_f32(xn, w_ref[:, d:2 * d]), gk_ref[...])
    vo_ref[...] = _dot_f32(xn, w_ref[:, 2 * d:3 * d])


def qkv_project(x, g, w, q_gain, k_gain, tm):
    t, d = x.shape
    n_heads = d // SB_HEADDIM
    head_of = jnp.arange(MXU_WIDTH) // SB_HEADDIM
    avg = jnp.where(head_of[:, None] == head_of[None, :], 1.0 / SB_HEADDIM, 0.0).astype(BF16)
    gq = jnp.tile(q_gain, n_heads).reshape(1, d)
    gk = jnp.tile(k_gain, n_heads).reshape(1, d)
    body = functools.partial(_qkv_body, scale=0.5 * SB_HEADDIM ** -0.5)
    row_spec = pl.BlockSpec((tm, d), lambda i: (i, 0))
    vec_spec = pl.BlockSpec((1, d), lambda i: (0, 0))
    return pl.pallas_call(
        body,
        grid=(t // tm,),
        in_specs=[row_spec, vec_spec,
                  pl.BlockSpec((d, 3 * d), lambda i: (0, 0)),
                  vec_spec, vec_spec,
                  pl.BlockSpec((MXU_WIDTH, MXU_WIDTH), lambda i: (0, 0))],
        out_specs=(row_spec, row_spec, row_spec),
        out_shape=(jax.ShapeDtypeStruct((t, d), BF16), jax.ShapeDtypeStruct((t, d), F32),
                   jax.ShapeDtypeStruct((t, d), F32)),
        compiler_params=_params("arbitrary"),
        name="qkv_project",
    )(x, g.reshape(1, d), w, gq, gk, avg)


def _kv_layout_body(k_ref, v_ref, perm_ref, kp_ref, vt_ref, *, blocks):
    pairs = kp_ref.shape[1]
    perm = perm_ref[...]
    for blk in range(blocks):
        rows = slice(blk * KEY_BLOCK, (blk + 1) * KEY_BLOCK)
        kp = _dot_f32(perm, k_ref[0, rows, :].astype(BF16))
        vp = _dot_f32(perm, v_ref[0, rows, :].astype(BF16))
        for p in range(pairs):
            lanes = slice(p * LANES, (p + 1) * LANES)
            kp_ref[0, p, blk] = kp[:, lanes].astype(BF16)
            vt_ref[0, p, blk] = vp[:, lanes].T.astype(BF16)


def kv_layout(k, v, v_col):
    b, s, d = k.shape
    nkb = s // KEY_BLOCK
    blocks = 8 if nkb % 8 == 0 else nkb
    pairs = d // LANES
    r = jnp.arange(KEY_BLOCK)
    key_of_row = (r % SUBLANES) * VREGS_PER_KEY_BLOCK + r // SUBLANES
    perm = (key_of_row[:, None] == r[None, :]).astype(BF16)
    out = jax.ShapeDtypeStruct((b, pairs, nkb, KEY_BLOCK, LANES), BF16)
    out_spec = pl.BlockSpec((1, pairs, blocks, KEY_BLOCK, LANES), lambda i, j: (i, 0, j, 0, 0))
    return pl.pallas_call(
        functools.partial(_kv_layout_body, blocks=blocks),
        grid=(b, nkb // blocks),
        in_specs=[pl.BlockSpec((1, blocks * KEY_BLOCK, d), lambda i, j: (i, j, 0)),
                  pl.BlockSpec((1, blocks * KEY_BLOCK, d), lambda i, j: (i, j, v_col)),
                  pl.BlockSpec((KEY_BLOCK, KEY_BLOCK), lambda i, j: (0, 0))],
        out_specs=(out_spec, out_spec),
        out_shape=(out, out),
        compiler_params=_params("arbitrary", "arbitrary"),
        name="kv_layout",
    )(k, v, perm)


def _attn_tile(qi, q_ref, k_ref, vt_ref, o_ref, zbuf_ref, wbuf_ref, *, tq, q_pos0):
    nv = VREGS_PER_KEY_BLOCK
    q = q_ref[0, qi * tq:(qi + 1) * tq, :]
    lane = lax.broadcasted_iota(jnp.int32, (tq, LANES), 1)
    qms = [jnp.where(lane // SB_HEADDIM == hh, q, jnp.zeros_like(q)) for hh in range(2)]
    r_io = lax.broadcasted_iota(jnp.int32, (KEY_BLOCK, KEY_BLOCK), 0)
    diag_mask = ((r_io % SUBLANES) * nv + r_io // SUBLANES
                 < lax.broadcasted_iota(jnp.int32, (KEY_BLOCK, KEY_BLOCK), 1))
    n_diag = tq // KEY_BLOCK
    n_clear = (q_pos0 + qi * tq) // KEY_BLOCK

    def shift_up(a, k):
        sub = lax.broadcasted_iota(jnp.int32, a.shape, 0)
        return jnp.where(sub < SUBLANES - k, pltpu.roll(a, SUBLANES - k, 0), 1.0)

    def scores(j0, n, hh):
        kb = k_ref[0, 0, pl.ds(j0, n)].reshape(n * KEY_BLOCK, LANES)
        return lax.dot_general(kb, qms[hh], (((1,), (1,)), ((), ())), preferred_element_type=F32)

    def values(j0, n, hh, w):
        last = k_ref.shape[2] - 1
        vt = jnp.concatenate([vt_ref[0, 0, jnp.minimum(j0 + r, last)][hh * SB_HEADDIM:(hh + 1) * SB_HEADDIM, :]
                              for r in range(n)], axis=1)
        return _dot_f32(vt, w)

    def weights(z, n, rest, diag):
        ws = [None] * n
        for r in range(n - 1, -1, -1):
            lo = r * KEY_BLOCK if diag else 0
            width = tq - lo
            om = 0.5 - 0.5 * jnp.tanh(z[r * KEY_BLOCK:(r + 1) * KEY_BLOCK, lo:])
            if diag:
                om_d = jnp.where(diag_mask, om[:, :KEY_BLOCK], 1.0)
                om = jnp.concatenate([om_d, om[:, KEY_BLOCK:]], axis=1) if width > KEY_BLOCK else om_d
            om = om.reshape(nv, SUBLANES, width)
            suffix = [None] * (nv + 1)
            suffix[nv] = 1.0
            suffix[nv - 1] = om[nv - 1]
            for v in range(nv - 2, -1, -1):
                suffix[v] = om[v] * suffix[v + 1]
            row_tot = suffix[0]
            later = shift_up(row_tot, 1)
            later = later * shift_up(later, 1)
            later = later * shift_up(later, 2)
            later = later * shift_up(later, 4)
            rest_r = rest[:, lo:]
            f = later * rest_r
            w = jnp.stack([(suffix[v + 1] - suffix[v]) * f for v in range(nv)],
                          axis=0).reshape(KEY_BLOCK, width).astype(BF16)
            tot = jnp.broadcast_to((row_tot * later)[0:1, :], (SUBLANES, width))
            if lo:
                w = jnp.concatenate([jnp.zeros((KEY_BLOCK, lo), BF16), w], axis=1)
                rest = jnp.concatenate([rest[:, :lo], rest_r * tot], axis=1)
            else:
                rest = rest_r * tot
            ws[r] = w
        return jnp.concatenate(ws, axis=0), rest

    nb = KEY_BLOCKS_PER_STEP
    n_far = n_clear // nb

    def group_start(t):
        return n_clear - nb * t

    ones = jnp.ones((SUBLANES, tq), F32)
    z0 = scores(n_clear, n_diag, 0)
    z1 = scores(n_clear, n_diag, 1)
    zbuf_ref[...] = scores(jnp.maximum(group_start(1), 0), nb, 0)
    w0, rest0 = weights(z0, n_diag, ones, True)
    acc0 = values(n_clear, n_diag, 0, w0)
    w1, rest1 = weights(z1, n_diag, ones, True)
    acc1 = jnp.zeros((SB_HEADDIM, tq), F32)
    if n_diag > nb:
        acc1 = values(n_clear, n_diag, 1, w1)
        wbuf_ref[...] = jnp.zeros(wbuf_ref.shape, BF16)
    else:
        if n_diag < nb:
            wbuf_ref[n_diag * KEY_BLOCK:, :] = jnp.zeros(((nb - n_diag) * KEY_BLOCK, tq), BF16)
        wbuf_ref[0:n_diag * KEY_BLOCK, :] = w1

    def far(t, carry):
        rest0, acc0, rest1, acc1 = carry
        j0 = group_start(t)
        acc1 = acc1 + values(j0 + nb, nb, 1, wbuf_ref[...])
        z1 = scores(j0, nb, 1)
        w0, rest0 = weights(zbuf_ref[...], nb, rest0, False)
        acc0 = acc0 + values(j0, nb, 0, w0)
        w1, rest1 = weights(z1, nb, rest1, False)
        wbuf_ref[...] = w1
        zbuf_ref[...] = scores(jnp.maximum(j0 - nb, 0), nb, 0)
        return rest0, acc0, rest1, acc1

    _, acc0, _, acc1 = lax.fori_loop(1, n_far + 1, far, (rest0, acc0, rest1, acc1), unroll=True)
    acc1 = acc1 + values(group_start(n_far), nb, 1, wbuf_ref[...])
    o_ref[0, qi * tq:(qi + 1) * tq, :] = jnp.concatenate([acc0, acc1], axis=0).T.astype(o_ref.dtype)


def _attn_body(q_ref, k_ref, vt_ref, o_ref, zbuf_ref, wbuf_ref, *, tq, q_pos0):
    for qi in range(q_ref.shape[1] // tq):
        _attn_tile(qi, q_ref, k_ref, vt_ref, o_ref, zbuf_ref, wbuf_ref, tq=tq, q_pos0=q_pos0)


def sb_attention(q, kp, vt, *, tq, q_pos0):
    b, sq, d = q.shape
    pairs, nkb = kp.shape[1], kp.shape[2]
    n_tiles = sq // tq
    for t in range(n_tiles):
        assert (q_pos0 + t * tq) % (KEY_BLOCK * KEY_BLOCKS_PER_STEP) == 0
    assert q_pos0 + sq <= nkb * KEY_BLOCK
    body = functools.partial(_attn_body, tq=tq, q_pos0=q_pos0)
    return pl.pallas_call(
        body,
        grid=(b, pairs),
        in_specs=[pl.BlockSpec((1, sq, LANES), lambda i, p: (i, 0, p)),
                  pl.BlockSpec((1, 1, nkb, KEY_BLOCK, LANES), lambda i, p: (i, p, 0, 0, 0)),
                  pl.BlockSpec((1, 1, nkb, LANES, KEY_BLOCK), lambda i, p: (i, p, 0, 0, 0))],
        out_specs=pl.BlockSpec((1, sq, LANES), lambda i, p: (i, 0, p)),
        out_shape=jax.ShapeDtypeStruct((b, sq, d), BF16),
        scratch_shapes=[pltpu.VMEM((KEY_BLOCKS_PER_STEP * KEY_BLOCK, tq), F32),
                        pltpu.VMEM((KEY_BLOCKS_PER_STEP * KEY_BLOCK, tq), BF16)],
        compiler_params=_params("arbitrary", "arbitrary"),
        name="sb_attention",
    )(q, kp, vt)


def _row_tile(t):
    return 1024 if t % 1024 == 0 else t


def _ssd_layer(y, b, s, conv_state, ssm_state, g_mix, in_w, conv_w, conv_b, dt_bias, a_log, d_skip,
               norm_w, out_w):
    t, d = y.shape
    n_heads = dt_bias.shape[0]
    n_main = in_w.shape[1] - n_heads
    w_dt = jnp.pad(in_w[:, n_main:], ((0, 0), (0, LANES - n_heads))).astype(BF16)
    zx, dt_raw = norm_matmul(y, g_mix, in_w[:, :n_main].astype(BF16), _row_tile(t), 1024,
                             out_dtype=BF16, w_side=w_dt)
    zx = zx.reshape(b, s, n_main)
    dt_raw = dt_raw.reshape(b, s, LANES)
    valid = SSD_CHUNK
    if s % SSD_CHUNK:
        assert s < SSD_CHUNK
        valid = s
        zx = jnp.pad(zx, ((0, 0), (0, SSD_CHUNK - s), (0, 0)))
        dt_raw = jnp.pad(dt_raw, ((0, 0), (0, SSD_CHUNK - s), (0, 0)))
    y_ssd, conv_out, ssm_out = ssd_scan(zx, dt_raw, conv_state, ssm_state, conv_w, conv_b, dt_bias,
                                        a_log, d_skip, norm_w, valid=valid)
    y_ssd = y_ssd[:, :s].reshape(t, -1)
    return matmul_res(y_ssd, out_w.astype(BF16), y, _row_tile(t)), conv_out, ssm_out


def _sb_layer(y, b, s, cache_k, cache_v, g_mix, qkv_w, q_gain, k_gain, out_w):
    t, d = y.shape
    tm = _row_tile(t)
    q, k, v = qkv_project(y, g_mix, qkv_w.astype(BF16), q_gain, k_gain, min(tm, 512))
    q = q.reshape(b, s, d)
    if cache_k is None:
        tq, q_pos0 = KEY_BLOCKS_PER_STEP * KEY_BLOCK, 0
        kp, vt = kv_layout(k.reshape(b, s, d), v.reshape(b, s, d), 0)
    else:
        past = cache_k.shape[1]
        tq, q_pos0 = KEY_BLOCK, past
        pad = ((0, 0), (0, tq - s), (0, 0))
        kp_c, vt_c = kv_layout(cache_k.reshape(b, past, d), cache_v.reshape(b, past, d), 0)
        kp_n, vt_n = kv_layout(jnp.pad(k.reshape(b, s, d), pad), jnp.pad(v.reshape(b, s, d), pad), 0)
        kp = jnp.concatenate([kp_c, kp_n], axis=2)
        vt = jnp.concatenate([vt_c, vt_n], axis=2)
        q = jnp.pad(q, pad)
    o = sb_attention(q, kp, vt, tq=tq, q_pos0=q_pos0)[:, :s].reshape(t, d)
    heads = d // SB_HEADDIM
    return (matmul_res(o, out_w.astype(BF16), y, tm),
            k.reshape(b, s, heads, SB_HEADDIM), v.reshape(b, s, heads, SB_HEADDIM))


def kernel(x_prompt, x_sample, state_ssm, state_conv, cache_k, cache_v, norm_mix, norm_mlp, ssd_in_w, ssd_conv_w, ssd_conv_b, ssd_dt_bias, ssd_a_log, ssd_d, ssd_norm_w, ssd_out_w, sb_qkv_w, sb_q_gain, sb_k_gain, sb_out_w, mlp_up, mlp_down):
    bp, sp, d = x_prompt.shape
    bs, ss, _ = x_sample.shape
    depth = norm_mix.shape[0]
    y_p = x_prompt.reshape(bp * sp, d)
    y_s = x_sample.reshape(bs * ss, d)
    p_ssm, p_conv, p_k, p_v = [], [], [], []
    s_ssm, s_conv, s_k, s_v = [], [], [], []
    for i in range(depth):
        j = i // 2
        if i % 2 == 0:
            prm = (norm_mix[i], ssd_in_w[j], ssd_conv_w[j], ssd_conv_b[j], ssd_dt_bias[j],
                   ssd_a_log[j], ssd_d[j], ssd_norm_w[j], ssd_out_w[j])
            y_p, c_p, st_p = _ssd_layer(y_p, bp, sp, None, None, *prm)
            y_s, c_s, st_s = _ssd_layer(y_s, bs, ss, state_conv[j], state_ssm[j], *prm)
            p_ssm.append(st_p); p_conv.append(c_p)
            s_ssm.append(st_s); s_conv.append(c_s)
        else:
            prm = (norm_mix[i], sb_qkv_w[j], sb_q_gain[j], sb_k_gain[j], sb_out_w[j])
            y_p, k_p, v_p = _sb_layer(y_p, bp, sp, None, None, *prm)
            y_s, k_s, v_s = _sb_layer(y_s, bs, ss, cache_k[j], cache_v[j], *prm)
            p_k.append(k_p); p_v.append(v_p)
            s_k.append(k_s); s_v.append(v_s)
        wu = mlp_up[i].astype(BF16)
        wd = mlp_down[i].astype(BF16)
        y_p = mlp(y_p, norm_mlp[i], wu, wd, _row_tile(y_p.shape[0]), 1024)
        y_s = mlp(y_s, norm_mlp[i], wu, wd, _row_tile(y_s.shape[0]), 1024)
    return (y_p.reshape(bp, sp, d), y_s.reshape(bs, ss, d),
            jnp.stack(p_ssm), jnp.stack(p_conv), jnp.stack(p_k), jnp.stack(p_v),
            jnp.stack(s_ssm), jnp.stack(s_conv), jnp.stack(s_k), jnp.stack(s_v))
```

```python
import functools

import jax
import jax.numpy as jnp
from jax import lax
from jax.experimental import pallas as pl
from jax.experimental.pallas import tpu as pltpu

F32 = jnp.float32
BF16 = jnp.bfloat16
EPS = 1e-6

LANES = 128
SUBLANES = 8
MXU_WIDTH = 256
VMEM_LIMIT_BYTES = 56 * 1024 * 1024

SSD_HEADDIM = 64
SSD_GROUPS = 8
SSD_STATE = 128
SSD_CONV = 4
SB_HEADDIM = 64
SSD_CHUNK = 128
SSD_CHUNKS_PER_STEP = 2
KEY_BLOCK = 128
VREGS_PER_KEY_BLOCK = KEY_BLOCK // SUBLANES
KEY_BLOCKS_PER_STEP = 4


def _params(*sem):
    return pltpu.CompilerParams(dimension_semantics=sem, vmem_limit_bytes=VMEM_LIMIT_BYTES)


def _rms_scale(x, g):
    ms = jnp.mean(x * x, axis=-1, keepdims=True)
    return x * lax.rsqrt(ms + EPS) * g


def _silu(x):
    h = 0.5 * x
    return h + h * jnp.tanh(h)


def _softplus(x):
    return jnp.maximum(x, 0.0) + jnp.log(1.0 + jnp.exp(-jnp.abs(x)))


def _split3(a):
    hi = a.astype(BF16)
    rem = a - hi.astype(F32)
    mid = rem.astype(BF16)
    lo = (rem - mid.astype(F32)).astype(BF16)
    return hi, mid, lo


def _dot_f32(a, b):
    return jnp.dot(a, b, preferred_element_type=F32)


def _norm_matmul_body(x_ref, g_ref, w_ref, *rest):
    side = len(rest) == 4
    if side:
        ws_ref, o_ref, os_ref, xn_ref = rest
    else:
        o_ref, xn_ref = rest

    @pl.when(pl.program_id(1) == 0)
    def _():
        xn_ref[...] = _rms_scale(x_ref[...], g_ref[...]).astype(BF16)
        if side:
            os_ref[...] = _dot_f32(xn_ref[...], ws_ref[...])

    o_ref[...] = _dot_f32(xn_ref[...], w_ref[...]).astype(o_ref.dtype)


def norm_matmul(x, g, w, tm, tn, out_dtype=F32, w_side=None):
    t, d = x.shape
    n = w.shape[1]
    in_specs = [pl.BlockSpec((tm, d), lambda i, j: (i, 0)),
                pl.BlockSpec((1, d), lambda i, j: (0, 0)),
                pl.BlockSpec((d, tn), lambda i, j: (0, j))]
    out_specs = pl.BlockSpec((tm, tn), lambda i, j: (i, j))
    out_shape = jax.ShapeDtypeStruct((t, n), out_dtype)
    args = [x, g.reshape(1, d), w]
    if w_side is not None:
        in_specs.append(pl.BlockSpec((d, LANES), lambda i, j: (0, 0)))
        out_specs = (out_specs, pl.BlockSpec((tm, LANES), lambda i, j: (i, 0)))
        out_shape = (out_shape, jax.ShapeDtypeStruct((t, LANES), F32))
        args.append(w_side)
    return pl.pallas_call(
        _norm_matmul_body,
        grid=(t // tm, n // tn),
        in_specs=in_specs,
        out_specs=out_specs,
        out_shape=out_shape,
        scratch_shapes=[pltpu.VMEM((tm, d), BF16)],
        compiler_params=_params("arbitrary", "arbitrary"),
        name="norm_matmul",
    )(*args)


def _matmul_res_body(a_ref, w_ref, r_ref, o_ref):
    o_ref[...] = r_ref[...] + _dot_f32(a_ref[...].astype(BF16), w_ref[...])


def matmul_res(a, w, res, tm):
    t, k = a.shape
    n = w.shape[1]
    return pl.pallas_call(
        _matmul_res_body,
        grid=(t // tm,),
        in_specs=[pl.BlockSpec((tm, k), lambda i: (i, 0)),
                  pl.BlockSpec((k, n), lambda i: (0, 0)),
                  pl.BlockSpec((tm, n), lambda i: (i, 0))],
        out_specs=pl.BlockSpec((tm, n), lambda i: (i, 0)),
        out_shape=jax.ShapeDtypeStruct((t, n), F32),
        compiler_params=_params("arbitrary"),
        name="matmul_res",
    )(a, w, res)


def _mlp_body(x_ref, g_ref, wu_ref, wd_ref, o_ref, xn_ref):
    @pl.when(pl.program_id(1) == 0)
    def _():
        x = x_ref[...]
        xn_ref[...] = _rms_scale(x, g_ref[...]).astype(BF16)
        o_ref[...] = x

    h = jnp.maximum(_dot_f32(xn_ref[...], wu_ref[...]), 0.0)
    o_ref[...] += _dot_f32((h * h).astype(BF16), wd_ref[...])


def mlp(x, g, wu, wd, tm, tf):
    t, d = x.shape
    f = wu.shape[1]
    return pl.pallas_call(
        _mlp_body,
        grid=(t // tm, f // tf),
        in_specs=[pl.BlockSpec((tm, d), lambda i, j: (i, 0)),
                  pl.BlockSpec((1, d), lambda i, j: (0, 0)),
                  pl.BlockSpec((d, tf), lambda i, j: (0, j)),
                  pl.BlockSpec((tf, d), lambda i, j: (j, 0))],
        out_specs=pl.BlockSpec((tm, d), lambda i, j: (i, 0)),
        out_shape=jax.ShapeDtypeStruct((t, d), F32),
        scratch_shapes=[pltpu.VMEM((tm, d), BF16)],
        compiler_params=_params("arbitrary", "arbitrary"),
        name="mlp",
    )(x, g.reshape(1, d), wu, wd)


def _ssd_body(*refs, chunks, **static):
    for cc in range(chunks):
        _ssd_chunk(*refs, row0=cc * SSD_CHUNK, last=cc == chunks - 1, **static)


def _ssd_chunk(*refs, row0, last, valid, zero_init, n_steps, n_heads):
    if zero_init:
        (z_ref, x_ref, b_ref, c_ref, dt_ref, cw_ref, cb_ref, dtb_ref, alog_ref, dsk_ref, nw_ref,
         hx_ref, sh_ref, y_ref, cso_ref, sso_ref, tail_ref, act_ref, ht_ref) = refs
        cs_ref = ss_ref = None
    else:
        (z_ref, x_ref, b_ref, c_ref, dt_ref, cw_ref, cb_ref, dtb_ref, alog_ref, dsk_ref, nw_ref,
         hx_ref, sh_ref, cs_ref, ss_ref, y_ref, cso_ref, sso_ref, tail_ref, act_ref, ht_ref) = refs
    L = SSD_CHUNK
    G = SSD_GROUPS
    R = n_heads // G
    P = SSD_HEADDIM
    GP = R * P
    d_inner = n_heads * P
    gn = G * SSD_STATE
    conv_dim = d_inner + 2 * gn
    rows = slice(row0, row0 + L)
    z_ref, x_ref, b_ref, c_ref, dt_ref, y_ref = (
        r.at[:, rows] for r in (z_ref, x_ref, b_ref, c_ref, dt_ref, y_ref))
    act_ref = act_ref.at[row0 // L]
    step = pl.program_id(1)

    @pl.when((step == 0) & (row0 == 0))
    def _init():
        tail_ref[...] = jnp.zeros(tail_ref.shape, F32)
        if zero_init:
            ht_ref[...] = jnp.zeros(ht_ref.shape, F32)
        else:
            tail_ref[SUBLANES - (SSD_CONV - 1):SUBLANES, :] = cs_ref[0]
            for g in range(G):
                ht_ref[g] = ss_ref[0, R * g:R * (g + 1)].reshape(GP, SSD_STATE).T

    col_chunk = 512
    for j in range(conv_dim // col_chunk):
        cols = slice(j * col_chunk, (j + 1) * col_chunk)
        if cols.stop <= d_inner:
            xb = x_ref[0, :, cols]
        elif cols.stop <= d_inner + gn:
            xb = b_ref[0, :, cols.start - d_inner:cols.stop - d_inner]
        else:
            xb = c_ref[0, :, cols.start - d_inner - gn:cols.stop - d_inner - gn]
        xf = xb.astype(F32)
        edge = jnp.concatenate([tail_ref[:, cols], xf[0:SUBLANES]], axis=0)
        acc = cb_ref[:, cols] + cw_ref[SSD_CONV - 1:SSD_CONV, cols] * xf
        for k in range(1, SSD_CONV):
            delayed = jnp.concatenate([pltpu.roll(edge, k, 0)[SUBLANES:],
                                       _dot_f32(sh_ref[k - 1], xb)[SUBLANES:]], axis=0)
            acc = acc + cw_ref[SSD_CONV - 1 - k:SSD_CONV - k, cols] * delayed
        act_ref[:, cols] = _silu(acc)
        tail_ref[:, cols] = xf[L - SUBLANES:]
        cso_ref[0, :, cols] = xf[valid - (SSD_CONV - 1):valid]

    dt = _softplus(dt_ref[0] + dtb_ref[...])
    t_io = lax.broadcasted_iota(jnp.int32, (L, L), 0)
    s_io = lax.broadcasted_iota(jnp.int32, (L, L), 1)
    causal = t_io >= s_io
    if valid < L:
        row = lax.broadcasted_iota(jnp.int32, (L, LANES), 0)
        dt = jnp.where(row < valid, dt, 0.0)
    adt = dt * (-jnp.exp(alog_ref[...]))
    tri = jnp.where(causal, 1.0, 0.0).astype(BF16)
    acs = sum(_dot_f32(tri, p) for p in _split3(adt))
    hx = hx_ref[...]
    e_x = jnp.exp(sum(_dot_f32(p, hx) for p in _split3(acs)))
    alast = acs[L - 1:L, :]
    acs_t = acs.T
    src_t = (acs - jnp.log(dt)).T
    u_t = (dt * jnp.exp(alast - acs)).T
    head_of_lane = (lax.broadcasted_iota(jnp.int32, (L, GP), 1) // P).astype(F32).astype(BF16)

    for g in range(G):
        gc = slice(g * GP, (g + 1) * GP)
        bg = act_ref[:, d_inner + g * SSD_STATE:d_inner + (g + 1) * SSD_STATE]
        cg = act_ref[:, d_inner + gn + g * SSD_STATE:d_inner + gn + (g + 1) * SSD_STATE]
        cbf = cg.astype(BF16)
        cb = lax.dot_general(cbf, bg.astype(BF16), (((1,), (1,)), ((), ())),
                             preferred_element_type=F32)
        bg_t = bg.T
        h_t = ht_ref[g]
        y_off = _dot_f32(cbf, h_t.astype(BF16)) * e_x[:, gc]
        xg = act_ref[:, gc]
        xb = xg.astype(BF16)
        x_bd = jnp.concatenate([jnp.where(head_of_lane == r, xb, jnp.zeros_like(xb))
                                for r in range(R)], axis=0)
        ms, bs = [], []
        for r in range(R):
            h = g * R + r
            seg = acs[:, h:h + 1] - src_t[h:h + 1, :]
            ms.append((jnp.where(causal, jnp.exp(seg), 0.0) * cb).astype(BF16))
            bs.append((bg_t * u_t[h:h + 1, :]).astype(BF16))
        y_d = _dot_f32(jnp.concatenate(ms, axis=1), x_bd)
        upd = _dot_f32(jnp.concatenate(bs, axis=1), x_bd)
        yg = (y_d + y_off + dsk_ref[:, gc] * xg) * _silu(z_ref[0, :, gc].astype(F32))
        y_ref[0, :, gc] = _rms_scale(yg, nw_ref[:, gc]).astype(y_ref.dtype)
        ht_ref[g] = h_t * e_x[L - 1:L, gc] + upd

    @pl.when((step == n_steps - 1) & last)
    def _state_out():
        for g in range(G):
            sso_ref[0, R * g:R * (g + 1)] = ht_ref[g].T.reshape(R, P, SSD_STATE)


def ssd_scan(zx, dt_raw, conv_state, ssm_state, conv_w, conv_b, dt_bias, a_log, d_skip, norm_w, *, valid):
    b, s, _ = zx.shape
    n_heads = dt_bias.shape[0]
    d_inner = n_heads * SSD_HEADDIM
    gn = SSD_GROUPS * SSD_STATE
    conv_dim = d_inner + 2 * gn
    L = SSD_CHUNK
    n_chunks = s // L
    chunks = SSD_CHUNKS_PER_STEP if n_chunks % SSD_CHUNKS_PER_STEP == 0 else 1
    rows = chunks * L
    zero_init = conv_state is None
    pad_h = LANES - n_heads
    head_expand = (jnp.arange(LANES)[:, None] == jnp.arange(d_inner)[None, :] // SSD_HEADDIM)
    consts = [conv_w, conv_b.reshape(1, conv_dim),
              jnp.pad(dt_bias, (0, pad_h)).reshape(1, LANES),
              jnp.pad(a_log, (0, pad_h)).reshape(1, LANES),
              jnp.repeat(d_skip, SSD_HEADDIM).reshape(1, d_inner),
              norm_w.reshape(1, d_inner),
              head_expand.astype(BF16)]
    const_specs = [pl.BlockSpec(a.shape, lambda i, j: (0, 0)) for a in consts]
    t_io = jnp.arange(L)
    shifts = jnp.stack([t_io[:, None] - k == t_io[None, :] for k in range(1, SSD_CONV)]).astype(BF16)
    consts.append(shifts)
    const_specs.append(pl.BlockSpec(shifts.shape, lambda i, j: (0, 0, 0)))
    in_specs = [pl.BlockSpec((1, rows, d_inner), lambda i, j: (i, j, 0)),
                pl.BlockSpec((1, rows, d_inner), lambda i, j: (i, j, 1)),
                pl.BlockSpec((1, rows, gn), lambda i, j: (i, j, 2 * d_inner // gn)),
                pl.BlockSpec((1, rows, gn), lambda i, j: (i, j, 2 * d_inner // gn + 1)),
                pl.BlockSpec((1, rows, LANES), lambda i, j: (i, j, 0)),
                ] + const_specs
    args = [zx, zx, zx, zx, dt_raw] + consts
    if not zero_init:
        in_specs += [pl.BlockSpec((1, SSD_CONV - 1, conv_dim), lambda i, j: (i, 0, 0)),
                     pl.BlockSpec((1, n_heads, SSD_HEADDIM, SSD_STATE), lambda i, j: (i, 0, 0, 0))]
        args += [conv_state, ssm_state]
    out_shape = (jax.ShapeDtypeStruct((b, s, d_inner), BF16),
                 jax.ShapeDtypeStruct((b, SSD_CONV - 1, conv_dim), F32),
                 jax.ShapeDtypeStruct((b, n_heads, SSD_HEADDIM, SSD_STATE), F32))
    out_specs = (pl.BlockSpec((1, rows, d_inner), lambda i, j: (i, j, 0)),
                 pl.BlockSpec((1, SSD_CONV - 1, conv_dim), lambda i, j: (i, 0, 0)),
                 pl.BlockSpec((1, n_heads, SSD_HEADDIM, SSD_STATE), lambda i, j: (i, 0, 0, 0)))
    body = functools.partial(_ssd_body, chunks=chunks, valid=valid, zero_init=zero_init,
                             n_steps=n_chunks // chunks, n_heads=n_heads)
    return pl.pallas_call(
        body,
        grid=(b, n_chunks // chunks),
        in_specs=in_specs,
        out_specs=out_specs,
        out_shape=out_shape,
        scratch_shapes=[pltpu.VMEM((SUBLANES, conv_dim), F32),
                        pltpu.VMEM((chunks, L, conv_dim), F32),
                        pltpu.VMEM((SSD_GROUPS, SSD_STATE, d_inner // SSD_GROUPS), F32)],
        compiler_params=_params("arbitrary", "arbitrary"),
        name="ssd_scan",
    )(*args)


def _qkv_body(*refs, scale, layout, aliased):
    x_ref, g_ref, w_ref, gq_ref, gk_ref, avg_ref = refs[:6]
    n_in = 6 + (1 if layout else 0) + (2 if aliased else 0)
    qo_ref, ko_ref, vo_ref = refs[n_in:n_in + 3]
    d = x_ref.shape[1]
    xn = _rms_scale(x_ref[...], g_ref[...]).astype(BF16)

    def head_norm(x, g):
        slab = avg_ref.shape[0]
        sq = (x * x).astype(BF16)
        ms = jnp.concatenate([_dot_f32(sq[:, c:c + slab], avg_ref[...])
                              for c in range(0, x.shape[1], slab)], axis=1)
        return x * lax.rsqrt(ms + EPS) * g

    qo_ref[...] = (head_norm(_dot_f32(xn, w_ref[:, 0:d]), gq_ref[...]) * scale).astype(qo_ref.dtype)
    k = head_norm(_dot_f32(xn, w_ref[:, d:2 * d]), gk_ref[...])
    v = _dot_f32(xn, w_ref[:, 2 * d:3 * d])
    ko_ref[0] = k
    vo_ref[0] = v
    if layout:
        _kv_layout_rows(k, v, refs[6][...], refs[n_in + 3], refs[n_in + 4])


def _kv_layout_rows(k, v, perm, kp_ref, vt_ref):
    pairs = kp_ref.shape[1]
    for blk in range(k.shape[0] // KEY_BLOCK):
        rows = slice(blk * KEY_BLOCK, (blk + 1) * KEY_BLOCK)
        kp = _dot_f32(perm, k[rows].astype(BF16))
        vp = _dot_f32(perm, v[rows].astype(BF16))
        for p in range(pairs):
            lanes = slice(p * LANES, (p + 1) * LANES)
            kp_ref[0, p, blk] = kp[:, lanes].astype(BF16)
            vt_ref[0, p, blk] = vp[:, lanes].T.astype(BF16)


def _key_permutation():
    r = jnp.arange(KEY_BLOCK)
    key_of_row = (r % SUBLANES) * VREGS_PER_KEY_BLOCK + r // SUBLANES
    return (key_of_row[:, None] == r[None, :]).astype(BF16)


def qkv_project(x, g, w, q_gain, k_gain, tm, *, slot, n_slots, stacks=None, seq=None):
    t, d = x.shape
    n_heads = d // SB_HEADDIM
    head_of = jnp.arange(MXU_WIDTH) // SB_HEADDIM
    avg = jnp.where(head_of[:, None] == head_of[None, :], 1.0 / SB_HEADDIM, 0.0).astype(BF16)
    gq = jnp.tile(q_gain, n_heads).reshape(1, d)
    gk = jnp.tile(k_gain, n_heads).reshape(1, d)
    layout = seq is not None
    aliased = stacks is not None
    body = functools.partial(_qkv_body, scale=0.5 * SB_HEADDIM ** -0.5, layout=layout, aliased=aliased)
    row_spec = pl.BlockSpec((tm, d), lambda i: (i, 0))
    vec_spec = pl.BlockSpec((1, d), lambda i: (0, 0))
    stack_spec = pl.BlockSpec((1, tm, d), lambda i: (slot, i, 0))
    stack_shape = jax.ShapeDtypeStruct((n_slots, t, d), F32)
    in_specs = [row_spec, vec_spec, pl.BlockSpec((d, 3 * d), lambda i: (0, 0)), vec_spec, vec_spec,
                pl.BlockSpec((MXU_WIDTH, MXU_WIDTH), lambda i: (0, 0))]
    args = [x, g.reshape(1, d), w, gq, gk, avg]
    out_specs = [row_spec, stack_spec, stack_spec]
    out_shape = [jax.ShapeDtypeStruct((t, d), BF16), stack_shape, stack_shape]
    if layout:
        assert tm % KEY_BLOCK == 0 and seq % tm == 0
        tiles, blocks, pairs = seq // tm, tm // KEY_BLOCK, d // LANES
        in_specs.append(pl.BlockSpec((KEY_BLOCK, KEY_BLOCK), lambda i: (0, 0)))
        args.append(_key_permutation())
        lay_spec = pl.BlockSpec((1, pairs, blocks, KEY_BLOCK, LANES),
                                lambda i: (i // tiles, 0, i % tiles, 0, 0))
        lay_shape = jax.ShapeDtypeStruct((t // seq, pairs, seq // KEY_BLOCK, KEY_BLOCK, LANES), BF16)
        out_specs += [lay_spec, lay_spec]
        out_shape += [lay_shape, lay_shape]
    aliases = {}
    if aliased:
        aliases = {len(args): 1, len(args) + 1: 2}
        in_specs += [pl.BlockSpec(memory_space=pl.ANY)] * 2
        args += list(stacks)
    return pl.pallas_call(
        body,
        grid=(t // tm,),
        in_specs=in_specs,
        out_specs=tuple(out_specs),
        out_shape=tuple(out_shape),
        input_output_aliases=aliases,
        compiler_params=_params("arbitrary"),
        name="qkv_project",
    )(*args)


def _kv_layout_body(k_ref, v_ref, perm_ref, kp_ref, vt_ref):
    _kv_layout_rows(k_ref[0], v_ref[0], perm_ref[...], kp_ref, vt_ref)


def kv_layout(k, v, v_col):
    b, s, d = k.shape
    nkb = s // KEY_BLOCK
    blocks = 8 if nkb % 8 == 0 else nkb
    pairs = d // LANES
    out = jax.ShapeDtypeStruct((b, pairs, nkb, KEY_BLOCK, LANES), BF16)
    out_spec = pl.BlockSpec((1, pairs, blocks, KEY_BLOCK, LANES), lambda i, j: (i, 0, j, 0, 0))
    return pl.pallas_call(
        _kv_layout_body,
        grid=(b, nkb // blocks),
        in_specs=[pl.BlockSpec((1, blocks * KEY_BLOCK, d), lambda i, j: (i, j, 0)),
                  pl.BlockSpec((1, blocks * KEY_BLOCK, d), lambda i, j: (i, j, v_col)),
                  pl.BlockSpec((KEY_BLOCK, KEY_BLOCK), lambda i, j: (0, 0))],
        out_specs=(out_spec, out_spec),
        out_shape=(out, out),
        compiler_params=_params("arbitrary", "arbitrary"),
        name="kv_layout",
    )(k, v, _key_permutation())


def _attn_tile(qi, q_ref, k_ref, vt_ref, o_ref, zbuf_ref, wbuf_ref, *, tq, q_pos0):
    nv = VREGS_PER_KEY_BLOCK
    q = q_ref[0, qi * tq:(qi + 1) * tq, :]
    lane = lax.broadcasted_iota(jnp.int32, (tq, LANES), 1)
    qms = [jnp.where(lane // SB_HEADDIM == hh, q, jnp.zeros_like(q)) for hh in range(2)]
    r_io = lax.broadcasted_iota(jnp.int32, (KEY_BLOCK, KEY_BLOCK), 0)
    diag_mask = ((r_io % SUBLANES) * nv + r_io // SUBLANES
                 < lax.broadcasted_iota(jnp.int32, (KEY_BLOCK, KEY_BLOCK), 1))
    n_diag = tq // KEY_BLOCK
    n_clear = (q_pos0 + qi * tq) // KEY_BLOCK

    def shift_up(a, k):
        sub = lax.broadcasted_iota(jnp.int32, a.shape, 0)
        return jnp.where(sub < SUBLANES - k, pltpu.roll(a, SUBLANES - k, 0), 1.0)

    def scores(j0, n, hh):
        kb = k_ref[0, 0, pl.ds(j0, n)].reshape(n * KEY_BLOCK, LANES)
        return lax.dot_general(kb, qms[hh], (((1,), (1,)), ((), ())), preferred_element_type=F32)

    def values(j0, n, hh, w):
        last = k_ref.shape[2] - 1
        vt = jnp.concatenate([vt_ref[0, 0, jnp.minimum(j0 + r, last)][hh * SB_HEADDIM:(hh + 1) * SB_HEADDIM, :]
                              for r in range(n)], axis=1)
        return _dot_f32(vt, w)

    def weights(z, n, rest, diag):
        ws = [None] * n
        for r in range(n - 1, -1, -1):
            lo = r * KEY_BLOCK if diag else 0
            width = tq - lo
            om = 0.5 - 0.5 * jnp.tanh(z[r * KEY_BLOCK:(r + 1) * KEY_BLOCK, lo:])
            if diag:
                om_d = jnp.where(diag_mask, om[:, :KEY_BLOCK], 1.0)
                om = jnp.concatenate([om_d, om[:, KEY_BLOCK:]], axis=1) if width > KEY_BLOCK else om_d
            om = om.reshape(nv, SUBLANES, width)
            suffix = [None] * (nv + 1)
            suffix[nv] = 1.0
            suffix[nv - 1] = om[nv - 1]
            for v in range(nv - 2, -1, -1):
                suffix[v] = om[v] * suffix[v + 1]
            row_tot = suffix[0]
            later = shift_up(row_tot, 1)
            later = later * shift_up(later, 1)
            later = later * shift_up(later, 2)
            later = later * shift_up(later, 4)
            rest_r = rest[:, lo:]
            f = later * rest_r
            w = jnp.stack([(suffix[v + 1] - suffix[v]) * f for v in range(nv)],
                          axis=0).reshape(KEY_BLOCK, width).astype(BF16)
            tot = jnp.broadcast_to((row_tot * later)[0:1, :], (SUBLANES, width))
            if lo:
                w = jnp.concatenate([jnp.zeros((KEY_BLOCK, lo), BF16), w], axis=1)
                rest = jnp.concatenate([rest[:, :lo], rest_r * tot], axis=1)
            else:
                rest = rest_r * tot
            ws[r] = w
        return jnp.concatenate(ws, axis=0), rest

    nb = KEY_BLOCKS_PER_STEP
    n_far = n_clear // nb

    def group_start(t):
        return n_clear - nb * t

    ones = jnp.ones((SUBLANES, tq), F32)
    z0 = scores(n_clear, n_diag, 0)
    z1 = scores(n_clear, n_diag, 1)
    zbuf_ref[...] = scores(jnp.maximum(group_start(1), 0), nb, 0)
    w0, rest0 = weights(z0, n_diag, ones, True)
    acc0 = values(n_clear, n_diag, 0, w0)
    w1, rest1 = weights(z1, n_diag, ones, True)
    acc1 = jnp.zeros((SB_HEADDIM, tq), F32)
    if n_diag > nb:
        acc1 = values(n_clear, n_diag, 1, w1)
        wbuf_ref[...] = jnp.zeros(wbuf_ref.shape, BF16)
    else:
        if n_diag < nb:
            wbuf_ref[n_diag * KEY_BLOCK:, :] = jnp.zeros(((nb - n_diag) * KEY_BLOCK, tq), BF16)
        wbuf_ref[0:n_diag * KEY_BLOCK, :] = w1

    def far(t, carry):
        rest0, acc0, rest1, acc1 = carry
        j0 = group_start(t)
        acc1 = acc1 + values(j0 + nb, nb, 1, wbuf_ref[...])
        z1 = scores(j0, nb, 1)
        w0, rest0 = weights(zbuf_ref[...], nb, rest0, False)
        acc0 = acc0 + values(j0, nb, 0, w0)
        w1, rest1 = weights(z1, nb, rest1, False)
        wbuf_ref[...] = w1
        zbuf_ref[...] = scores(jnp.maximum(j0 - nb, 0), nb, 0)
        return rest0, acc0, rest1, acc1

    _, acc0, _, acc1 = lax.fori_loop(1, n_far + 1, far, (rest0, acc0, rest1, acc1), unroll=True)
    acc1 = acc1 + values(group_start(n_far), nb, 1, wbuf_ref[...])
    o_ref[0, qi * tq:(qi + 1) * tq, :] = jnp.concatenate([acc0, acc1], axis=0).T.astype(o_ref.dtype)


def _attn_body(q_ref, k_ref, vt_ref, o_ref, zbuf_ref, wbuf_ref, *, tq, q_pos0):
    for qi in range(q_ref.shape[1] // tq):
        _attn_tile(qi, q_ref, k_ref, vt_ref, o_ref, zbuf_ref, wbuf_ref, tq=tq, q_pos0=q_pos0)


def sb_attention(q, kp, vt, *, tq, q_pos0):
    b, sq, d = q.shape
    pairs, nkb = kp.shape[1], kp.shape[2]
    n_tiles = sq // tq
    for t in range(n_tiles):
        assert (q_pos0 + t * tq) % (KEY_BLOCK * KEY_BLOCKS_PER_STEP) == 0
    assert q_pos0 + sq <= nkb * KEY_BLOCK
    body = functools.partial(_attn_body, tq=tq, q_pos0=q_pos0)
    return pl.pallas_call(
        body,
        grid=(b, pairs),
        in_specs=[pl.BlockSpec((1, sq, LANES), lambda i, p: (i, 0, p)),
                  pl.BlockSpec((1, 1, nkb, KEY_BLOCK, LANES), lambda i, p: (i, p, 0, 0, 0)),
                  pl.BlockSpec((1, 1, nkb, LANES, KEY_BLOCK), lambda i, p: (i, p, 0, 0, 0))],
        out_specs=pl.BlockSpec((1, sq, LANES), lambda i, p: (i, 0, p)),
        out_shape=jax.ShapeDtypeStruct((b, sq, d), BF16),
        scratch_shapes=[pltpu.VMEM((KEY_BLOCKS_PER_STEP * KEY_BLOCK, tq), F32),
                        pltpu.VMEM((KEY_BLOCKS_PER_STEP * KEY_BLOCK, tq), BF16)],
        compiler_params=_params("arbitrary", "arbitrary"),
        name="sb_attention",
    )(q, kp, vt)


def _row_tile(t):
    return 1024 if t % 1024 == 0 else t


def _ssd_layer(y, b, s, conv_state, ssm_state, g_mix, in_w, conv_w, conv_b, dt_bias, a_log, d_skip,
               norm_w, out_w):
    t, d = y.shape
    n_heads = dt_bias.shape[0]
    n_main = in_w.shape[1] - n_heads
    w_dt = jnp.pad(in_w[:, n_main:], ((0, 0), (0, LANES - n_heads))).astype(BF16)
    zx, dt_raw = norm_matmul(y, g_mix, in_w[:, :n_main].astype(BF16), _row_tile(t), 1024,
                             out_dtype=BF16, w_side=w_dt)
    zx = zx.reshape(b, s, n_main)
    dt_raw = dt_raw.reshape(b, s, LANES)
    valid = SSD_CHUNK
    if s % SSD_CHUNK:
        assert s < SSD_CHUNK
        valid = s
        zx = jnp.pad(zx, ((0, 0), (0, SSD_CHUNK - s), (0, 0)))
        dt_raw = jnp.pad(dt_raw, ((0, 0), (0, SSD_CHUNK - s), (0, 0)))
    y_ssd, conv_out, ssm_out = ssd_scan(zx, dt_raw, conv_state, ssm_state, conv_w, conv_b, dt_bias,
                                        a_log, d_skip, norm_w, valid=valid)
    y_ssd = y_ssd[:, :s].reshape(t, -1)
    return matmul_res(y_ssd, out_w.astype(BF16), y, _row_tile(t)), conv_out, ssm_out


def _sb_layer(y, b, s, cache_k, cache_v, slot, n_slots, stacks, g_mix, qkv_w, q_gain, k_gain, out_w):
    t, d = y.shape
    tm = _row_tile(t)
    w = qkv_w.astype(BF16)
    if cache_k is None:
        tq, q_pos0 = KEY_BLOCKS_PER_STEP * KEY_BLOCK, 0
        q, ks, vs, kp, vt = qkv_project(y, g_mix, w, q_gain, k_gain, min(tm, 512), slot=slot,
                                        n_slots=n_slots, stacks=stacks, seq=s)
        q = q.reshape(b, s, d)
    else:
        past = cache_k.shape[1]
        tq, q_pos0 = KEY_BLOCK, past
        q, ks, vs = qkv_project(y, g_mix, w, q_gain, k_gain, min(tm, 512), slot=slot,
                                n_slots=n_slots, stacks=stacks)
        pad = ((0, 0), (0, tq - s), (0, 0))
        kp_c, vt_c = kv_layout(cache_k.reshape(b, past, d), cache_v.reshape(b, past, d), 0)
        kp_n, vt_n = kv_layout(jnp.pad(ks[slot].reshape(b, s, d), pad),
                               jnp.pad(vs[slot].reshape(b, s, d), pad), 0)
        kp = jnp.concatenate([kp_c, kp_n], axis=2)
        vt = jnp.concatenate([vt_c, vt_n], axis=2)
        q = jnp.pad(q.reshape(b, s, d), pad)
    o = sb_attention(q, kp, vt, tq=tq, q_pos0=q_pos0)[:, :s].reshape(t, d)
    return matmul_res(o, out_w.astype(BF16), y, tm), (ks, vs)


def kernel(x_prompt, x_sample, state_ssm, state_conv, cache_k, cache_v, norm_mix, norm_mlp, ssd_in_w, ssd_conv_w, ssd_conv_b, ssd_dt_bias, ssd_a_log, ssd_d, ssd_norm_w, ssd_out_w, sb_qkv_w, sb_q_gain, sb_k_gain, sb_out_w, mlp_up, mlp_down):
    bp, sp, d = x_prompt.shape
    bs, ss, _ = x_sample.shape
    depth = norm_mix.shape[0]
    y_p = x_prompt.reshape(bp * sp, d)
    y_s = x_sample.reshape(bs * ss, d)
    n_sb = sb_qkv_w.shape[0]
    heads = d // SB_HEADDIM
    p_ssm, p_conv, s_ssm, s_conv = [], [], [], []
    p_kv = s_kv = None
    for i in range(depth):
        j = i // 2
        if i % 2 == 0:
            prm = (norm_mix[i], ssd_in_w[j], ssd_conv_w[j], ssd_conv_b[j], ssd_dt_bias[j],
                   ssd_a_log[j], ssd_d[j], ssd_norm_w[j], ssd_out_w[j])
            y_p, c_p, st_p = _ssd_layer(y_p, bp, sp, None, None, *prm)
            y_s, c_s, st_s = _ssd_layer(y_s, bs, ss, state_conv[j], state_ssm[j], *prm)
            p_ssm.append(st_p); p_conv.append(c_p)
            s_ssm.append(st_s); s_conv.append(c_s)
        else:
            prm = (norm_mix[i], sb_qkv_w[j], sb_q_gain[j], sb_k_gain[j], sb_out_w[j])
            y_p, p_kv = _sb_layer(y_p, bp, sp, None, None, j, n_sb, p_kv, *prm)
            y_s, s_kv = _sb_layer(y_s, bs, ss, cache_k[j], cache_v[j], j, n_sb, s_kv, *prm)
        wu = mlp_up[i].astype(BF16)
        wd = mlp_down[i].astype(BF16)
        y_p = mlp(y_p, norm_mlp[i], wu, wd, _row_tile(y_p.shape[0]), 1024)
        y_s = mlp(y_s, norm_mlp[i], wu, wd, _row_tile(y_s.shape[0]), 1024)
    p_k, p_v = (a.reshape(n_sb, bp, sp, heads, SB_HEADDIM) for a in p_kv)
    s_k, s_v = (a.reshape(n_sb, bs, ss, heads, SB_HEADDIM) for a in s_kv)
    return (y_p.reshape(bp, sp, d), y_s.reshape(bs, ss, d),
            jnp.stack(p_ssm), jnp.stack(p_conv), p_k, p_v,
            jnp.stack(s_ssm), jnp.stack(s_conv), s_k, s_v)
```

```python
import functools

import jax
import jax.numpy as jnp
from jax import lax
from jax.experimental import pallas as pl
from jax.experimental.pallas import tpu as pltpu

F32 = jnp.float32
BF16 = jnp.bfloat16
EPS = 1e-6

LANES = 128
SUBLANES = 8
MXU_WIDTH = 256
VMEM_LIMIT_BYTES = 56 * 1024 * 1024

SSD_HEADDIM = 64
SSD_GROUPS = 8
SSD_STATE = 128
SSD_CONV = 4
SB_HEADDIM = 64
SSD_CHUNK = 128
SSD_CHUNKS_PER_STEP = 4
KEY_BLOCK = 128
VREGS_PER_KEY_BLOCK = KEY_BLOCK // SUBLANES
KEY_BLOCKS_PER_STEP = 4


def _params(*sem):
    return pltpu.CompilerParams(dimension_semantics=sem, vmem_limit_bytes=VMEM_LIMIT_BYTES)


def _rms_scale(x, g):
    ms = jnp.mean(x * x, axis=-1, keepdims=True)
    return x * lax.rsqrt(ms + EPS) * g


def _silu(x):
    h = 0.5 * x
    return h + h * jnp.tanh(h)


def _softplus(x):
    return jnp.maximum(x, 0.0) + jnp.log(1.0 + jnp.exp(-jnp.abs(x)))


def _split3(a):
    hi = a.astype(BF16)
    rem = a - hi.astype(F32)
    mid = rem.astype(BF16)
    lo = (rem - mid.astype(F32)).astype(BF16)
    return hi, mid, lo


def _dot_f32(a, b):
    return jnp.dot(a, b, preferred_element_type=F32)


def _norm_matmul_body(x_ref, g_ref, w_ref, *rest):
    side = len(rest) == 4
    if side:
        ws_ref, o_ref, os_ref, xn_ref = rest
    else:
        o_ref, xn_ref = rest

    @pl.when(pl.program_id(1) == 0)
    def _():
        xn_ref[...] = _rms_scale(x_ref[...], g_ref[...]).astype(BF16)
        if side:
            os_ref[...] = _dot_f32(xn_ref[...], ws_ref[...])

    o_ref[...] = _dot_f32(xn_ref[...], w_ref[...]).astype(o_ref.dtype)


def norm_matmul(x, g, w, tm, tn, out_dtype=F32, w_side=None):
    t, d = x.shape
    n = w.shape[1]
    in_specs = [pl.BlockSpec((tm, d), lambda i, j: (i, 0)),
                pl.BlockSpec((1, d), lambda i, j: (0, 0)),
                pl.BlockSpec((d, tn), lambda i, j: (0, j))]
    out_specs = pl.BlockSpec((tm, tn), lambda i, j: (i, j))
    out_shape = jax.ShapeDtypeStruct((t, n), out_dtype)
    args = [x, g.reshape(1, d), w]
    if w_side is not None:
        in_specs.append(pl.BlockSpec((d, LANES), lambda i, j: (0, 0)))
        out_specs = (out_specs, pl.BlockSpec((tm, LANES), lambda i, j: (i, 0)))
        out_shape = (out_shape, jax.ShapeDtypeStruct((t, LANES), F32))
        args.append(w_side)
    return pl.pallas_call(
        _norm_matmul_body,
        grid=(t // tm, n // tn),
        in_specs=in_specs,
        out_specs=out_specs,
        out_shape=out_shape,
        scratch_shapes=[pltpu.VMEM((tm, d), BF16)],
        compiler_params=_params("arbitrary", "arbitrary"),
        name="norm_matmul",
    )(*args)


def _matmul_res_body(a_ref, w_ref, r_ref, o_ref):
    o_ref[...] = r_ref[...] + _dot_f32(a_ref[...].astype(BF16), w_ref[...])


def matmul_res(a, w, res, tm):
    t, k = a.shape
    n = w.shape[1]
    return pl.pallas_call(
        _matmul_res_body,
        grid=(t // tm,),
        in_specs=[pl.BlockSpec((tm, k), lambda i: (i, 0)),
                  pl.BlockSpec((k, n), lambda i: (0, 0)),
                  pl.BlockSpec((tm, n), lambda i: (i, 0))],
        out_specs=pl.BlockSpec((tm, n), lambda i: (i, 0)),
        out_shape=jax.ShapeDtypeStruct((t, n), F32),
        compiler_params=_params("arbitrary"),
        name="matmul_res",
    )(a, w, res)


def _mlp_body(x_ref, g_ref, wu_ref, wd_ref, o_ref, xn_ref):
    @pl.when(pl.program_id(1) == 0)
    def _():
        x = x_ref[...]
        xn_ref[...] = _rms_scale(x, g_ref[...]).astype(BF16)
        o_ref[...] = x

    h = jnp.maximum(_dot_f32(xn_ref[...], wu_ref[...]), 0.0)
    o_ref[...] += _dot_f32((h * h).astype(BF16), wd_ref[...])


def mlp(x, g, wu, wd, tm, tf):
    t, d = x.shape
    f = wu.shape[1]
    return pl.pallas_call(
        _mlp_body,
        grid=(t // tm, f // tf),
        in_specs=[pl.BlockSpec((tm, d), lambda i, j: (i, 0)),
                  pl.BlockSpec((1, d), lambda i, j: (0, 0)),
                  pl.BlockSpec((d, tf), lambda i, j: (0, j)),
                  pl.BlockSpec((tf, d), lambda i, j: (j, 0))],
        out_specs=pl.BlockSpec((tm, d), lambda i, j: (i, 0)),
        out_shape=jax.ShapeDtypeStruct((t, d), F32),
        scratch_shapes=[pltpu.VMEM((tm, d), BF16)],
        compiler_params=_params("arbitrary", "arbitrary"),
        name="mlp",
    )(x, g.reshape(1, d), wu, wd)


def _ssd_body(*refs, chunks, **static):
    for cc in range(chunks):
        _ssd_chunk(*refs, row0=cc * SSD_CHUNK, last=cc == chunks - 1, **static)


def _ssd_chunk(*refs, row0, last, valid, zero_init, n_steps, n_heads):
    if zero_init:
        (z_ref, x_ref, b_ref, c_ref, dt_ref, cw_ref, cb_ref, dtb_ref, alog_ref, dsk_ref, nw_ref,
         hx_ref, sh_ref, y_ref, cso_ref, sso_ref, tail_ref, act_ref, ht_ref) = refs
        cs_ref = ss_ref = None
    else:
        (z_ref, x_ref, b_ref, c_ref, dt_ref, cw_ref, cb_ref, dtb_ref, alog_ref, dsk_ref, nw_ref,
         hx_ref, sh_ref, cs_ref, ss_ref, y_ref, cso_ref, sso_ref, tail_ref, act_ref, ht_ref) = refs
    L = SSD_CHUNK
    G = SSD_GROUPS
    R = n_heads // G
    P = SSD_HEADDIM
    GP = R * P
    d_inner = n_heads * P
    gn = G * SSD_STATE
    conv_dim = d_inner + 2 * gn
    rows = slice(row0, row0 + L)
    z_ref, x_ref, b_ref, c_ref, dt_ref, y_ref = (
        r.at[:, rows] for r in (z_ref, x_ref, b_ref, c_ref, dt_ref, y_ref))
    act_ref = act_ref.at[row0 // L]
    step = pl.program_id(1)

    @pl.when((step == 0) & (row0 == 0))
    def _init():
        tail_ref[...] = jnp.zeros(tail_ref.shape, F32)
        if zero_init:
            ht_ref[...] = jnp.zeros(ht_ref.shape, F32)
        else:
            tail_ref[SUBLANES - (SSD_CONV - 1):SUBLANES, :] = cs_ref[0]
            for g in range(G):
                ht_ref[g] = ss_ref[0, R * g:R * (g + 1)].reshape(GP, SSD_STATE).T

    col_chunk = 512
    for j in range(conv_dim // col_chunk):
        cols = slice(j * col_chunk, (j + 1) * col_chunk)
        if cols.stop <= d_inner:
            xb = x_ref[0, :, cols]
        elif cols.stop <= d_inner + gn:
            xb = b_ref[0, :, cols.start - d_inner:cols.stop - d_inner]
        else:
            xb = c_ref[0, :, cols.start - d_inner - gn:cols.stop - d_inner - gn]
        xf = xb.astype(F32)
        edge = jnp.concatenate([tail_ref[:, cols], xf[0:SUBLANES]], axis=0)
        acc = cb_ref[:, cols] + cw_ref[SSD_CONV - 1:SSD_CONV, cols] * xf
        for k in range(1, SSD_CONV):
            delayed = jnp.concatenate([pltpu.roll(edge, k, 0)[SUBLANES:],
                                       _dot_f32(sh_ref[k - 1], xb)[SUBLANES:]], axis=0)
            acc = acc + cw_ref[SSD_CONV - 1 - k:SSD_CONV - k, cols] * delayed
        act_ref[:, cols] = _silu(acc)
        tail_ref[:, cols] = xf[L - SUBLANES:]
        cso_ref[0, :, cols] = xf[valid - (SSD_CONV - 1):valid]

    dt = _softplus(dt_ref[0] + dtb_ref[...])
    t_io = lax.broadcasted_iota(jnp.int32, (L, L), 0)
    s_io = lax.broadcasted_iota(jnp.int32, (L, L), 1)
    causal = t_io >= s_io
    if valid < L:
        row = lax.broadcasted_iota(jnp.int32, (L, LANES), 0)
        dt = jnp.where(row < valid, dt, 0.0)
    adt = dt * (-jnp.exp(alog_ref[...]))
    tri = jnp.where(causal, 1.0, 0.0).astype(BF16)
    acs = sum(_dot_f32(tri, p) for p in _split3(adt))
    hx = hx_ref[...]
    e_x = jnp.exp(sum(_dot_f32(p, hx) for p in _split3(acs)))
    alast = acs[L - 1:L, :]
    acs_t = acs.T
    src_t = (acs - jnp.log(dt)).T
    u_t = (dt * jnp.exp(alast - acs)).T
    head_of_lane = (lax.broadcasted_iota(jnp.int32, (L, GP), 1) // P).astype(F32).astype(BF16)

    for g in range(G):
        gc = slice(g * GP, (g + 1) * GP)
        bg = act_ref[:, d_inner + g * SSD_STATE:d_inner + (g + 1) * SSD_STATE]
        cg = act_ref[:, d_inner + gn + g * SSD_STATE:d_inner + gn + (g + 1) * SSD_STATE]
        cbf = cg.astype(BF16)
        cb = lax.dot_general(cbf, bg.astype(BF16), (((1,), (1,)), ((), ())),
                             preferred_element_type=F32)
        bg_t = bg.T
        h_t = ht_ref[g]
        y_off = _dot_f32(cbf, h_t.astype(BF16)) * e_x[:, gc]
        xg = act_ref[:, gc]
        xb = xg.astype(BF16)
        x_bd = jnp.concatenate([jnp.where(head_of_lane == r, xb, jnp.zeros_like(xb))
                                for r in range(R)], axis=0)
        ms, bs = [], []
        for r in range(R):
            h = g * R + r
            seg = acs[:, h:h + 1] - src_t[h:h + 1, :]
            ms.append((jnp.where(causal, jnp.exp(seg), 0.0) * cb).astype(BF16))
            bs.append((bg_t * u_t[h:h + 1, :]).astype(BF16))
        y_d = _dot_f32(jnp.concatenate(ms, axis=1), x_bd)
        upd = _dot_f32(jnp.concatenate(bs, axis=1), x_bd)
        yg = (y_d + y_off + dsk_ref[:, gc] * xg) * _silu(z_ref[0, :, gc].astype(F32))
        y_ref[0, :, gc] = _rms_scale(yg, nw_ref[:, gc]).astype(y_ref.dtype)
        ht_ref[g] = h_t * e_x[L - 1:L, gc] + upd

    @pl.when((step == n_steps - 1) & last)
    def _state_out():
        for g in range(G):
            sso_ref[0, R * g:R * (g + 1)] = ht_ref[g].T.reshape(R, P, SSD_STATE)


def ssd_scan(zx, dt_raw, conv_state, ssm_state, conv_w, conv_b, dt_bias, a_log, d_skip, norm_w, *, valid):
    b, s, _ = zx.shape
    n_heads = dt_bias.shape[0]
    d_inner = n_heads * SSD_HEADDIM
    gn = SSD_GROUPS * SSD_STATE
    conv_dim = d_inner + 2 * gn
    L = SSD_CHUNK
    n_chunks = s // L
    chunks = SSD_CHUNKS_PER_STEP if n_chunks % SSD_CHUNKS_PER_STEP == 0 else 1
    rows = chunks * L
    zero_init = conv_state is None
    pad_h = LANES - n_heads
    head_expand = (jnp.arange(LANES)[:, None] == jnp.arange(d_inner)[None, :] // SSD_HEADDIM)
    consts = [conv_w, conv_b.reshape(1, conv_dim),
              jnp.pad(dt_bias, (0, pad_h)).reshape(1, LANES),
              jnp.pad(a_log, (0, pad_h)).reshape(1, LANES),
              jnp.repeat(d_skip, SSD_HEADDIM).reshape(1, d_inner),
              norm_w.reshape(1, d_inner),
              head_expand.astype(BF16)]
    const_specs = [pl.BlockSpec(a.shape, lambda i, j: (0, 0)) for a in consts]
    t_io = jnp.arange(L)
    shifts = jnp.stack([t_io[:, None] - k == t_io[None, :] for k in range(1, SSD_CONV)]).astype(BF16)
    consts.append(shifts)
    const_specs.append(pl.BlockSpec(shifts.shape, lambda i, j: (0, 0, 0)))
    in_specs = [pl.BlockSpec((1, rows, d_inner), lambda i, j: (i, j, 0)),
                pl.BlockSpec((1, rows, d_inner), lambda i, j: (i, j, 1)),
                pl.BlockSpec((1, rows, gn), lambda i, j: (i, j, 2 * d_inner // gn)),
                pl.BlockSpec((1, rows, gn), lambda i, j: (i, j, 2 * d_inner // gn + 1)),
                pl.BlockSpec((1, rows, LANES), lambda i, j: (i, j, 0)),
                ] + const_specs
    args = [zx, zx, zx, zx, dt_raw] + consts
    if not zero_init:
        in_specs += [pl.BlockSpec((1, SSD_CONV - 1, conv_dim), lambda i, j: (i, 0, 0)),
                     pl.BlockSpec((1, n_heads, SSD_HEADDIM, SSD_STATE), lambda i, j: (i, 0, 0, 0))]
        args += [conv_state, ssm_state]
    out_shape = (jax.ShapeDtypeStruct((b, s, d_inner), BF16),
                 jax.ShapeDtypeStruct((b, SSD_CONV - 1, conv_dim), F32),
                 jax.ShapeDtypeStruct((b, n_heads, SSD_HEADDIM, SSD_STATE), F32))
    out_specs = (pl.BlockSpec((1, rows, d_inner), lambda i, j: (i, j, 0)),
                 pl.BlockSpec((1, SSD_CONV - 1, conv_dim), lambda i, j: (i, 0, 0)),
                 pl.BlockSpec((1, n_heads, SSD_HEADDIM, SSD_STATE), lambda i, j: (i, 0, 0, 0)))
    body = functools.partial(_ssd_body, chunks=chunks, valid=valid, zero_init=zero_init,
                             n_steps=n_chunks // chunks, n_heads=n_heads)
    return pl.pallas_call(
        body,
        grid=(b, n_chunks // chunks),
        in_specs=in_specs,
        out_specs=out_specs,
        out_shape=out_shape,
        scratch_shapes=[pltpu.VMEM((SUBLANES, conv_dim), F32),
                        pltpu.VMEM((chunks, L, conv_dim), F32),
                        pltpu.VMEM((SSD_GROUPS, SSD_STATE, d_inner // SSD_GROUPS), F32)],
        compiler_params=_params("arbitrary", "arbitrary"),
        name="ssd_scan",
    )(*args)


def _qkv_body(*refs, scale, layout, aliased):
    x_ref, g_ref, w_ref, gq_ref, gk_ref, avg_ref = refs[:6]
    n_in = 6 + (1 if layout else 0) + (2 if aliased else 0)
    qo_ref, ko_ref, vo_ref = refs[n_in:n_in + 3]
    d = x_ref.shape[1]
    xn = _rms_scale(x_ref[...], g_ref[...]).astype(BF16)

    def head_norm(x, g):
        slab = avg_ref.shape[0]
        sq = (x * x).astype(BF16)
        ms = jnp.concatenate([_dot_f32(sq[:, c:c + slab], avg_ref[...])
                              for c in range(0, x.shape[1], slab)], axis=1)
        return x * lax.rsqrt(ms + EPS) * g

    qo_ref[...] = (head_norm(_dot_f32(xn, w_ref[:, 0:d]), gq_ref[...]) * scale).astype(qo_ref.dtype)
    k = head_norm(_dot_f32(xn, w_ref[:, d:2 * d]), gk_ref[...])
    v = _dot_f32(xn, w_ref[:, 2 * d:3 * d])
    ko_ref[0] = k
    vo_ref[0] = v
    if layout:
        _kv_layout_rows(k, v, refs[6][...], refs[n_in + 3], refs[n_in + 4])


def _kv_layout_rows(k, v, perm, kp_ref, vt_ref):
    pairs = kp_ref.shape[1]
    for blk in range(k.shape[0] // KEY_BLOCK):
        rows = slice(blk * KEY_BLOCK, (blk + 1) * KEY_BLOCK)
        kp = _dot_f32(perm, k[rows].astype(BF16))
        vp = _dot_f32(perm, v[rows].astype(BF16))
        for p in range(pairs):
            lanes = slice(p * LANES, (p + 1) * LANES)
            kp_ref[0, p, blk] = kp[:, lanes].astype(BF16)
            vt_ref[0, p, blk] = vp[:, lanes].T.astype(BF16)


def _key_permutation():
    r = jnp.arange(KEY_BLOCK)
    key_of_row = (r % SUBLANES) * VREGS_PER_KEY_BLOCK + r // SUBLANES
    return (key_of_row[:, None] == r[None, :]).astype(BF16)


def qkv_project(x, g, w, q_gain, k_gain, tm, *, slot, n_slots, stacks=None, seq=None):
    t, d = x.shape
    n_heads = d // SB_HEADDIM
    head_of = jnp.arange(MXU_WIDTH) // SB_HEADDIM
    avg = jnp.where(head_of[:, None] == head_of[None, :], 1.0 / SB_HEADDIM, 0.0).astype(BF16)
    gq = jnp.tile(q_gain, n_heads).reshape(1, d)
    gk = jnp.tile(k_gain, n_heads).reshape(1, d)
    layout = seq is not None
    aliased = stacks is not None
    body = functools.partial(_qkv_body, scale=0.5 * SB_HEADDIM ** -0.5, layout=layout, aliased=aliased)
    row_spec = pl.BlockSpec((tm, d), lambda i: (i, 0))
    vec_spec = pl.BlockSpec((1, d), lambda i: (0, 0))
    stack_spec = pl.BlockSpec((1, tm, d), lambda i: (slot, i, 0))
    stack_shape = jax.ShapeDtypeStruct((n_slots, t, d), F32)
    in_specs = [row_spec, vec_spec, pl.BlockSpec((d, 3 * d), lambda i: (0, 0)), vec_spec, vec_spec,
                pl.BlockSpec((MXU_WIDTH, MXU_WIDTH), lambda i: (0, 0))]
    args = [x, g.reshape(1, d), w, gq, gk, avg]
    out_specs = [row_spec, stack_spec, stack_spec]
    out_shape = [jax.ShapeDtypeStruct((t, d), BF16), stack_shape, stack_shape]
    if layout:
        assert tm % KEY_BLOCK == 0 and seq % tm == 0
        tiles, blocks, pairs = seq // tm, tm // KEY_BLOCK, d // LANES
        in_specs.append(pl.BlockSpec((KEY_BLOCK, KEY_BLOCK), lambda i: (0, 0)))
        args.append(_key_permutation())
        lay_spec = pl.BlockSpec((1, pairs, blocks, KEY_BLOCK, LANES),
                                lambda i: (i // tiles, 0, i % tiles, 0, 0))
        lay_shape = jax.ShapeDtypeStruct((t // seq, pairs, seq // KEY_BLOCK, KEY_BLOCK, LANES), BF16)
        out_specs += [lay_spec, lay_spec]
        out_shape += [lay_shape, lay_shape]
    aliases = {}
    if aliased:
        aliases = {len(args): 1, len(args) + 1: 2}
        in_specs += [pl.BlockSpec(memory_space=pl.ANY)] * 2
        args += list(stacks)
    return pl.pallas_call(
        body,
        grid=(t // tm,),
        in_specs=in_specs,
        out_specs=tuple(out_specs),
        out_shape=tuple(out_shape),
        input_output_aliases=aliases,
        compiler_params=_params("arbitrary"),
        name="qkv_project",
    )(*args)


def _kv_layout_body(k_ref, v_ref, perm_ref, kp_ref, vt_ref):
    _kv_layout_rows(k_ref[0], v_ref[0], perm_ref[...], kp_ref, vt_ref)


def kv_layout(k, v, b, batch0=0):
    _, s, d = k.shape
    nkb = s // KEY_BLOCK
    blocks = 8 if nkb % 8 == 0 else nkb
    pairs = d // LANES
    out = jax.ShapeDtypeStruct((b, pairs, nkb, KEY_BLOCK, LANES), BF16)
    out_spec = pl.BlockSpec((1, pairs, blocks, KEY_BLOCK, LANES), lambda i, j: (i, 0, j, 0, 0))
    return pl.pallas_call(
        _kv_layout_body,
        grid=(b, nkb // blocks),
        in_specs=[pl.BlockSpec((1, blocks * KEY_BLOCK, d), lambda i, j: (i + batch0, j, 0)),
                  pl.BlockSpec((1, blocks * KEY_BLOCK, d), lambda i, j: (i + batch0, j, 0)),
                  pl.BlockSpec((KEY_BLOCK, KEY_BLOCK), lambda i, j: (0, 0))],
        out_specs=(out_spec, out_spec),
        out_shape=(out, out),
        compiler_params=_params("arbitrary", "arbitrary"),
        name="kv_layout",
    )(k, v, _key_permutation())


def _attn_tile(qi, q_ref, k_ref, vt_ref, o_ref, zbuf_ref, wbuf_ref, *, tq, q_pos0):
    nv = VREGS_PER_KEY_BLOCK
    q = q_ref[0, qi * tq:(qi + 1) * tq, :]
    lane = lax.broadcasted_iota(jnp.int32, (tq, LANES), 1)
    qms = [jnp.where(lane // SB_HEADDIM == hh, q, jnp.zeros_like(q)) for hh in range(2)]
    r_io = lax.broadcasted_iota(jnp.int32, (KEY_BLOCK, KEY_BLOCK), 0)
    diag_mask = ((r_io % SUBLANES) * nv + r_io // SUBLANES
                 < lax.broadcasted_iota(jnp.int32, (KEY_BLOCK, KEY_BLOCK), 1))
    n_diag = tq // KEY_BLOCK
    n_clear = (q_pos0 + qi * tq) // KEY_BLOCK

    def shift_up(a, k):
        sub = lax.broadcasted_iota(jnp.int32, a.shape, 0)
        return jnp.where(sub < SUBLANES - k, pltpu.roll(a, SUBLANES - k, 0), 1.0)

    def scores(j0, n, hh):
        kb = k_ref[0, 0, pl.ds(j0, n)].reshape(n * KEY_BLOCK, LANES)
        return lax.dot_general(kb, qms[hh], (((1,), (1,)), ((), ())), preferred_element_type=F32)

    def values(j0, n, hh, w):
        last = k_ref.shape[2] - 1
        vt = jnp.concatenate([vt_ref[0, 0, jnp.minimum(j0 + r, last)][hh * SB_HEADDIM:(hh + 1) * SB_HEADDIM, :]
                              for r in range(n)], axis=1)
        return _dot_f32(vt, w)

    def weights(z, n, rest, diag):
        ws = [None] * n
        for r in range(n - 1, -1, -1):
            lo = r * KEY_BLOCK if diag else 0
            width = tq - lo
            om = 0.5 - 0.5 * jnp.tanh(z[r * KEY_BLOCK:(r + 1) * KEY_BLOCK, lo:])
            if diag:
                om_d = jnp.where(diag_mask, om[:, :KEY_BLOCK], 1.0)
                om = jnp.concatenate([om_d, om[:, KEY_BLOCK:]], axis=1) if width > KEY_BLOCK else om_d
            om = om.reshape(nv, SUBLANES, width)
            suffix = [None] * (nv + 1)
            suffix[nv] = 1.0
            suffix[nv - 1] = om[nv - 1]
            for v in range(nv - 2, -1, -1):
                suffix[v] = om[v] * suffix[v + 1]
            row_tot = suffix[0]
            later = shift_up(row_tot, 1)
            later = later * shift_up(later, 1)
            later = later * shift_up(later, 2)
            later = later * shift_up(later, 4)
            rest_r = rest[:, lo:]
            f = later * rest_r
            w = jnp.stack([(suffix[v + 1] - suffix[v]) * f for v in range(nv)],
                          axis=0).reshape(KEY_BLOCK, width).astype(BF16)
            tot = jnp.broadcast_to((row_tot * later)[0:1, :], (SUBLANES, width))
            if lo:
                w = jnp.concatenate([jnp.zeros((KEY_BLOCK, lo), BF16), w], axis=1)
                rest = jnp.concatenate([rest[:, :lo], rest_r * tot], axis=1)
            else:
                rest = rest_r * tot
            ws[r] = w
        return jnp.concatenate(ws, axis=0), rest

    nb = KEY_BLOCKS_PER_STEP
    n_far = n_clear // nb

    def group_start(t):
        return n_clear - nb * t

    ones = jnp.ones((SUBLANES, tq), F32)
    z0 = scores(n_clear, n_diag, 0)
    z1 = scores(n_clear, n_diag, 1)
    zbuf_ref[...] = scores(jnp.maximum(group_start(1), 0), nb, 0)
    w0, rest0 = weights(z0, n_diag, ones, True)
    acc0 = values(n_clear, n_diag, 0, w0)
    w1, rest1 = weights(z1, n_diag, ones, True)
    acc1 = jnp.zeros((SB_HEADDIM, tq), F32)
    if n_diag > nb:
        acc1 = values(n_clear, n_diag, 1, w1)
        wbuf_ref[...] = jnp.zeros(wbuf_ref.shape, BF16)
    else:
        if n_diag < nb:
            wbuf_ref[n_diag * KEY_BLOCK:, :] = jnp.zeros(((nb - n_diag) * KEY_BLOCK, tq), BF16)
        wbuf_ref[0:n_diag * KEY_BLOCK, :] = w1

    def far(t, carry):
        rest0, acc0, rest1, acc1 = carry
        j0 = group_start(t)
        acc1 = acc1 + values(j0 + nb, nb, 1, wbuf_ref[...])
        z1 = scores(j0, nb, 1)
        w0, rest0 = weights(zbuf_ref[...], nb, rest0, False)
        acc0 = acc0 + values(j0, nb, 0, w0)
        w1, rest1 = weights(z1, nb, rest1, False)
        wbuf_ref[...] = w1
        zbuf_ref[...] = scores(jnp.maximum(j0 - nb, 0), nb, 0)
        return rest0, acc0, rest1, acc1

    _, acc0, _, acc1 = lax.fori_loop(1, n_far + 1, far, (rest0, acc0, rest1, acc1), unroll=True)
    acc1 = acc1 + values(group_start(n_far), nb, 1, wbuf_ref[...])
    o_ref[0, qi * tq:(qi + 1) * tq, :] = jnp.concatenate([acc0, acc1], axis=0).T.astype(o_ref.dtype)


def _attn_body(q_ref, k_ref, vt_ref, o_ref, zbuf_ref, wbuf_ref, *, tq, q_pos0):
    for qi in range(q_ref.shape[1] // tq):
        _attn_tile(qi, q_ref, k_ref, vt_ref, o_ref, zbuf_ref, wbuf_ref, tq=tq, q_pos0=q_pos0)


def sb_attention(q, kp, vt, *, tq, q_pos0):
    b, sq, d = q.shape
    pairs, nkb = kp.shape[1], kp.shape[2]
    n_tiles = sq // tq
    for t in range(n_tiles):
        assert (q_pos0 + t * tq) % (KEY_BLOCK * KEY_BLOCKS_PER_STEP) == 0
    assert q_pos0 + sq <= nkb * KEY_BLOCK
    body = functools.partial(_attn_body, tq=tq, q_pos0=q_pos0)
    return pl.pallas_call(
        body,
        grid=(b, pairs),
        in_specs=[pl.BlockSpec((1, sq, LANES), lambda i, p: (i, 0, p)),
                  pl.BlockSpec((1, 1, nkb, KEY_BLOCK, LANES), lambda i, p: (i, p, 0, 0, 0)),
                  pl.BlockSpec((1, 1, nkb, LANES, KEY_BLOCK), lambda i, p: (i, p, 0, 0, 0))],
        out_specs=pl.BlockSpec((1, sq, LANES), lambda i, p: (i, 0, p)),
        out_shape=jax.ShapeDtypeStruct((b, sq, d), BF16),
        scratch_shapes=[pltpu.VMEM((KEY_BLOCKS_PER_STEP * KEY_BLOCK, tq), F32),
                        pltpu.VMEM((KEY_BLOCKS_PER_STEP * KEY_BLOCK, tq), BF16)],
        compiler_params=_params("arbitrary", "arbitrary"),
        name="sb_attention",
    )(q, kp, vt)


def _row_tile(t):
    return 1024 if t % 1024 == 0 else t


def _ssd_layer(y, b, s, conv_state, ssm_state, g_mix, in_w, conv_w, conv_b, dt_bias, a_log, d_skip,
               norm_w, out_w):
    t, d = y.shape
    n_heads = dt_bias.shape[0]
    n_main = in_w.shape[1] - n_heads
    w_dt = jnp.pad(in_w[:, n_main:], ((0, 0), (0, LANES - n_heads))).astype(BF16)
    zx, dt_raw = norm_matmul(y, g_mix, in_w[:, :n_main].astype(BF16), _row_tile(t), 2048,
                             out_dtype=BF16, w_side=w_dt)
    zx = zx.reshape(b, s, n_main)
    dt_raw = dt_raw.reshape(b, s, LANES)
    valid = SSD_CHUNK
    if s % SSD_CHUNK:
        assert s < SSD_CHUNK
        valid = s
        zx = jnp.pad(zx, ((0, 0), (0, SSD_CHUNK - s), (0, 0)))
        dt_raw = jnp.pad(dt_raw, ((0, 0), (0, SSD_CHUNK - s), (0, 0)))
    y_ssd, conv_out, ssm_out = ssd_scan(zx, dt_raw, conv_state, ssm_state, conv_w, conv_b, dt_bias,
                                        a_log, d_skip, norm_w, valid=valid)
    y_ssd = y_ssd[:, :s].reshape(t, -1)
    return matmul_res(y_ssd, out_w.astype(BF16), y, _row_tile(t)), conv_out, ssm_out


def _sb_layer(y, b, s, cache_k, cache_v, slot, n_slots, stacks, g_mix, qkv_w, q_gain, k_gain, out_w):
    t, d = y.shape
    tm = _row_tile(t)
    w = qkv_w.astype(BF16)
    if cache_k is None:
        tq, q_pos0 = KEY_BLOCKS_PER_STEP * KEY_BLOCK, 0
        q, ks, vs, kp, vt = qkv_project(y, g_mix, w, q_gain, k_gain, min(tm, 512), slot=slot,
                                        n_slots=n_slots, stacks=stacks, seq=s)
        q = q.reshape(b, s, d)
    else:
        past = cache_k.shape[1]
        tq, q_pos0 = KEY_BLOCK, past
        q, ks, vs = qkv_project(y, g_mix, w, q_gain, k_gain, min(tm, 512), slot=slot,
                                n_slots=n_slots, stacks=stacks)
        pad = ((0, 0), (0, tq - s), (0, 0))
        kp_c, vt_c = kv_layout(cache_k, cache_v, b, batch0=slot * b)
        kp_n, vt_n = kv_layout(jnp.pad(ks[slot].reshape(b, s, d), pad),
                               jnp.pad(vs[slot].reshape(b, s, d), pad), b)
        kp = jnp.concatenate([kp_c, kp_n], axis=2)
        vt = jnp.concatenate([vt_c, vt_n], axis=2)
        q = jnp.pad(q.reshape(b, s, d), pad)
    o = sb_attention(q, kp, vt, tq=tq, q_pos0=q_pos0)[:, :s].reshape(t, d)
    return matmul_res(o, out_w.astype(BF16), y, tm), (ks, vs)


def kernel(x_prompt, x_sample, state_ssm, state_conv, cache_k, cache_v, norm_mix, norm_mlp, ssd_in_w, ssd_conv_w, ssd_conv_b, ssd_dt_bias, ssd_a_log, ssd_d, ssd_norm_w, ssd_out_w, sb_qkv_w, sb_q_gain, sb_k_gain, sb_out_w, mlp_up, mlp_down):
    bp, sp, d = x_prompt.shape
    bs, ss, _ = x_sample.shape
    depth = norm_mix.shape[0]
    y_p = x_prompt.reshape(bp * sp, d)
    y_s = x_sample.reshape(bs * ss, d)
    n_sb = sb_qkv_w.shape[0]
    heads = d // SB_HEADDIM
    p_ssm, p_conv, s_ssm, s_conv = [], [], [], []
    p_kv = s_kv = None
    past = cache_k.shape[2]
    cache_k2 = cache_k.reshape(n_sb * bs, past, d)
    cache_v2 = cache_v.reshape(n_sb * bs, past, d)
    for i in range(depth):
        j = i // 2
        if i % 2 == 0:
            prm = (norm_mix[i], ssd_in_w[j], ssd_conv_w[j], ssd_conv_b[j], ssd_dt_bias[j],
                   ssd_a_log[j], ssd_d[j], ssd_norm_w[j], ssd_out_w[j])
            y_p, c_p, st_p = _ssd_layer(y_p, bp, sp, None, None, *prm)
            y_s, c_s, st_s = _ssd_layer(y_s, bs, ss, state_conv[j], state_ssm[j], *prm)
            p_ssm.append(st_p); p_conv.append(c_p)
            s_ssm.append(st_s); s_conv.append(c_s)
        else:
            prm = (norm_mix[i], sb_qkv_w[j], sb_q_gain[j], sb_k_gain[j], sb_out_w[j])
            y_p, p_kv = _sb_layer(y_p, bp, sp, None, None, j, n_sb, p_kv, *prm)
            y_s, s_kv = _sb_layer(y_s, bs, ss, cache_k2, cache_v2, j, n_sb, s_kv, *prm)
        wu = mlp_up[i].astype(BF16)
        wd = mlp_down[i].astype(BF16)
        y_p = mlp(y_p, norm_mlp[i], wu, wd, _row_tile(y_p.shape[0]), 1024)
        y_s = mlp(y_s, norm_mlp[i], wu, wd, _row_tile(y_s.shape[0]), 1024)
    p_k, p_v = (a.reshape(n_sb, bp, sp, heads, SB_HEADDIM) for a in p_kv)
    s_k, s_v = (a.reshape(n_sb, bs, ss, heads, SB_HEADDIM) for a in s_kv)
    return (y_p.reshape(bp, sp, d), y_s.reshape(bs, ss, d),
            jnp.stack(p_ssm), jnp.stack(p_conv), p_k, p_v,
            jnp.stack(s_ssm), jnp.stack(s_conv), s_k, s_v)
```

```python
import functools

import jax
import jax.numpy as jnp
from jax import lax
from jax.experimental import pallas as pl
from jax.experimental.pallas import tpu as pltpu

F32 = jnp.float32
BF16 = jnp.bfloat16
EPS = 1e-6

LANES = 128
SUBLANES = 8
MXU_WIDTH = 256
VMEM_LIMIT_BYTES = 56 * 1024 * 1024

SSD_HEADDIM = 64
SSD_GROUPS = 8
SSD_STATE = 128
SSD_CONV = 4
SB_HEADDIM = 64
SSD_CHUNK = 128
SSD_CHUNKS_PER_STEP = 4
KEY_BLOCK = 128
VREGS_PER_KEY_BLOCK = KEY_BLOCK // SUBLANES
KEY_BLOCKS_PER_STEP = 4


def _params(*sem):
    return pltpu.CompilerParams(dimension_semantics=sem, vmem_limit_bytes=VMEM_LIMIT_BYTES)


def _rms_scale(x, g):
    ms = jnp.mean(x * x, axis=-1, keepdims=True)
    return x * lax.rsqrt(ms + EPS) * g


def _silu(x):
    h = 0.5 * x
    return h + h * jnp.tanh(h)


def _softplus(x):
    return jnp.maximum(x, 0.0) + jnp.log(1.0 + jnp.exp(-jnp.abs(x)))


def _split3(a):
    hi = a.astype(BF16)
    rem = a - hi.astype(F32)
    mid = rem.astype(BF16)
    lo = (rem - mid.astype(F32)).astype(BF16)
    return hi, mid, lo


def _dot_f32(a, b):
    return jnp.dot(a, b, preferred_element_type=F32)


def _norm_matmul_body(x_ref, g_ref, w_ref, *rest):
    side = len(rest) == 4
    if side:
        ws_ref, o_ref, os_ref, xn_ref = rest
    else:
        o_ref, xn_ref = rest

    @pl.when(pl.program_id(1) == 0)
    def _():
        xn_ref[...] = _rms_scale(x_ref[...], g_ref[...]).astype(BF16)
        if side:
            os_ref[...] = _dot_f32(xn_ref[...], ws_ref[...])

    o_ref[...] = _dot_f32(xn_ref[...], w_ref[...]).astype(o_ref.dtype)


def norm_matmul(x, g, w, tm, tn, out_dtype=F32, w_side=None):
    t, d = x.shape
    n = w.shape[1]
    in_specs = [pl.BlockSpec((tm, d), lambda i, j: (i, 0)),
                pl.BlockSpec((1, d), lambda i, j: (0, 0)),
                pl.BlockSpec((d, tn), lambda i, j: (0, j))]
    out_specs = pl.BlockSpec((tm, tn), lambda i, j: (i, j))
    out_shape = jax.ShapeDtypeStruct((t, n), out_dtype)
    args = [x, g.reshape(1, d), w]
    if w_side is not None:
        in_specs.append(pl.BlockSpec((d, LANES), lambda i, j: (0, 0)))
        out_specs = (out_specs, pl.BlockSpec((tm, LANES), lambda i, j: (i, 0)))
        out_shape = (out_shape, jax.ShapeDtypeStruct((t, LANES), F32))
        args.append(w_side)
    return pl.pallas_call(
        _norm_matmul_body,
        grid=(t // tm, n // tn),
        in_specs=in_specs,
        out_specs=out_specs,
        out_shape=out_shape,
        scratch_shapes=[pltpu.VMEM((tm, d), BF16)],
        compiler_params=_params("arbitrary", "arbitrary"),
        name="norm_matmul",
    )(*args)


def _matmul_res_body(a_ref, w_ref, r_ref, o_ref):
    o_ref[...] = r_ref[...] + _dot_f32(a_ref[...].astype(BF16), w_ref[...])


def matmul_res(a, w, res, tm):
    t, k = a.shape
    n = w.shape[1]
    return pl.pallas_call(
        _matmul_res_body,
        grid=(t // tm,),
        in_specs=[pl.BlockSpec((tm, k), lambda i: (i, 0)),
                  pl.BlockSpec((k, n), lambda i: (0, 0)),
                  pl.BlockSpec((tm, n), lambda i: (i, 0))],
        out_specs=pl.BlockSpec((tm, n), lambda i: (i, 0)),
        out_shape=jax.ShapeDtypeStruct((t, n), F32),
        compiler_params=_params("arbitrary"),
        name="matmul_res",
    )(a, w, res)


def _mlp_body(x_ref, g_ref, wu_ref, wd_ref, o_ref, xn_ref):
    @pl.when(pl.program_id(1) == 0)
    def _():
        x = x_ref[...]
        xn_ref[...] = _rms_scale(x, g_ref[...]).astype(BF16)
        o_ref[...] = x

    h = jnp.maximum(_dot_f32(xn_ref[...], wu_ref[...]), 0.0)
    o_ref[...] += _dot_f32((h * h).astype(BF16), wd_ref[...])


def mlp(x, g, wu, wd, tm, tf):
    t, d = x.shape
    f = wu.shape[1]
    return pl.pallas_call(
        _mlp_body,
        grid=(t // tm, f // tf),
        in_specs=[pl.BlockSpec((tm, d), lambda i, j: (i, 0)),
                  pl.BlockSpec((1, d), lambda i, j: (0, 0)),
                  pl.BlockSpec((d, tf), lambda i, j: (0, j)),
                  pl.BlockSpec((tf, d), lambda i, j: (j, 0))],
        out_specs=pl.BlockSpec((tm, d), lambda i, j: (i, 0)),
        out_shape=jax.ShapeDtypeStruct((t, d), F32),
        scratch_shapes=[pltpu.VMEM((tm, d), BF16)],
        compiler_params=_params("arbitrary", "arbitrary"),
        name="mlp",
    )(x, g.reshape(1, d), wu, wd)


def _ssd_body(*refs, chunks, **static):
    for cc in range(chunks):
        _ssd_chunk(*refs, row0=cc * SSD_CHUNK, last=cc == chunks - 1, **static)


def _ssd_chunk(*refs, row0, last, valid, zero_init, n_steps, n_heads):
    if zero_init:
        (z_ref, x_ref, b_ref, c_ref, dt_ref, cw_ref, cb_ref, dtb_ref, alog_ref, dsk_ref, nw_ref,
         hx_ref, sh_ref, y_ref, cso_ref, sso_ref, tail_ref, act_ref, ht_ref) = refs
        cs_ref = ss_ref = None
    else:
        (z_ref, x_ref, b_ref, c_ref, dt_ref, cw_ref, cb_ref, dtb_ref, alog_ref, dsk_ref, nw_ref,
         hx_ref, sh_ref, cs_ref, ss_ref, y_ref, cso_ref, sso_ref, tail_ref, act_ref, ht_ref) = refs
    L = SSD_CHUNK
    G = SSD_GROUPS
    R = n_heads // G
    P = SSD_HEADDIM
    GP = R * P
    d_inner = n_heads * P
    gn = G * SSD_STATE
    conv_dim = d_inner + 2 * gn
    rows = slice(row0, row0 + L)
    z_ref, x_ref, b_ref, c_ref, dt_ref, y_ref = (
        r.at[:, rows] for r in (z_ref, x_ref, b_ref, c_ref, dt_ref, y_ref))
    act_ref = act_ref.at[row0 // L]
    step = pl.program_id(1)

    @pl.when((step == 0) & (row0 == 0))
    def _init():
        tail_ref[...] = jnp.zeros(tail_ref.shape, F32)
        if zero_init:
            ht_ref[...] = jnp.zeros(ht_ref.shape, F32)
        else:
            tail_ref[SUBLANES - (SSD_CONV - 1):SUBLANES, :] = cs_ref[0]
            for g in range(G):
                ht_ref[g] = ss_ref[0, R * g:R * (g + 1)].reshape(GP, SSD_STATE).T

    col_chunk = 512
    for j in range(conv_dim // col_chunk):
        cols = slice(j * col_chunk, (j + 1) * col_chunk)
        if cols.stop <= d_inner:
            xb = x_ref[0, :, cols]
        elif cols.stop <= d_inner + gn:
            xb = b_ref[0, :, cols.start - d_inner:cols.stop - d_inner]
        else:
            xb = c_ref[0, :, cols.start - d_inner - gn:cols.stop - d_inner - gn]
        xf = xb.astype(F32)
        edge = jnp.concatenate([tail_ref[:, cols], xf[0:SUBLANES]], axis=0)
        acc = cb_ref[:, cols] + cw_ref[SSD_CONV - 1:SSD_CONV, cols] * xf
        for k in range(1, SSD_CONV):
            delayed = jnp.concatenate([pltpu.roll(edge, k, 0)[SUBLANES:],
                                       _dot_f32(sh_ref[k - 1], xb)[SUBLANES:]], axis=0)
            acc = acc + cw_ref[SSD_CONV - 1 - k:SSD_CONV - k, cols] * delayed
        act_ref[:, cols] = _silu(acc)
        tail_ref[:, cols] = xf[L - SUBLANES:]
        cso_ref[0, :, cols] = xf[valid - (SSD_CONV - 1):valid]

    dt = _softplus(dt_ref[0] + dtb_ref[...])
    t_io = lax.broadcasted_iota(jnp.int32, (L, L), 0)
    s_io = lax.broadcasted_iota(jnp.int32, (L, L), 1)
    causal = t_io >= s_io
    if valid < L:
        row = lax.broadcasted_iota(jnp.int32, (L, LANES), 0)
        dt = jnp.where(row < valid, dt, 0.0)
    adt = dt * (-jnp.exp(alog_ref[...]))
    tri = jnp.where(causal, 1.0, 0.0).astype(BF16)
    acs = sum(_dot_f32(tri, p) for p in _split3(adt))
    hx = hx_ref[...]
    e_x = jnp.exp(sum(_dot_f32(p, hx) for p in _split3(acs)))
    alast = acs[L - 1:L, :]
    acs_t = acs.T
    src_t = (acs - jnp.log(dt)).T
    u_t = (dt * jnp.exp(alast - acs)).T
    head_of_lane = (lax.broadcasted_iota(jnp.int32, (L, GP), 1) // P).astype(F32).astype(BF16)

    for g in range(G):
        gc = slice(g * GP, (g + 1) * GP)
        bg = act_ref[:, d_inner + g * SSD_STATE:d_inner + (g + 1) * SSD_STATE]
        cg = act_ref[:, d_inner + gn + g * SSD_STATE:d_inner + gn + (g + 1) * SSD_STATE]
        cbf = cg.astype(BF16)
        cb = lax.dot_general(cbf, bg.astype(BF16), (((1,), (1,)), ((), ())),
                             preferred_element_type=F32)
        bg_t = bg.T
        h_t = ht_ref[g]
        y_off = _dot_f32(cbf, h_t.astype(BF16)) * e_x[:, gc]
        xg = act_ref[:, gc]
        xb = xg.astype(BF16)
        x_bd = jnp.concatenate([jnp.where(head_of_lane == r, xb, jnp.zeros_like(xb))
                                for r in range(R)], axis=0)
        ms, bs = [], []
        for r in range(R):
            h = g * R + r
            seg = acs[:, h:h + 1] - src_t[h:h + 1, :]
            ms.append((jnp.where(causal, jnp.exp(seg), 0.0) * cb).astype(BF16))
            bs.append((bg_t * u_t[h:h + 1, :]).astype(BF16))
        y_d = _dot_f32(jnp.concatenate(ms, axis=1), x_bd)
        upd = _dot_f32(jnp.concatenate(bs, axis=1), x_bd)
        yg = (y_d + y_off + dsk_ref[:, gc] * xg) * _silu(z_ref[0, :, gc].astype(F32))
        y_ref[0, :, gc] = _rms_scale(yg, nw_ref[:, gc]).astype(y_ref.dtype)
        ht_ref[g] = h_t * e_x[L - 1:L, gc] + upd

    @pl.when((step == n_steps - 1) & last)
    def _state_out():
        for g in range(G):
            sso_ref[0, R * g:R * (g + 1)] = ht_ref[g].T.reshape(R, P, SSD_STATE)


def ssd_scan(zx, dt_raw, conv_state, ssm_state, conv_w, conv_b, dt_bias, a_log, d_skip, norm_w, *, valid):
    b, s, _ = zx.shape
    n_heads = dt_bias.shape[0]
    d_inner = n_heads * SSD_HEADDIM
    gn = SSD_GROUPS * SSD_STATE
    conv_dim = d_inner + 2 * gn
    L = SSD_CHUNK
    n_chunks = s // L
    chunks = SSD_CHUNKS_PER_STEP if n_chunks % SSD_CHUNKS_PER_STEP == 0 else 1
    rows = chunks * L
    zero_init = conv_state is None
    pad_h = LANES - n_heads
    head_expand = (jnp.arange(LANES)[:, None] == jnp.arange(d_inner)[None, :] // SSD_HEADDIM)
    consts = [conv_w, conv_b.reshape(1, conv_dim),
              jnp.pad(dt_bias, (0, pad_h)).reshape(1, LANES),
              jnp.pad(a_log, (0, pad_h)).reshape(1, LANES),
              jnp.repeat(d_skip, SSD_HEADDIM).reshape(1, d_inner),
              norm_w.reshape(1, d_inner),
              head_expand.astype(BF16)]
    const_specs = [pl.BlockSpec(a.shape, lambda i, j: (0, 0)) for a in consts]
    t_io = jnp.arange(L)
    shifts = jnp.stack([t_io[:, None] - k == t_io[None, :] for k in range(1, SSD_CONV)]).astype(BF16)
    consts.append(shifts)
    const_specs.append(pl.BlockSpec(shifts.shape, lambda i, j: (0, 0, 0)))
    in_specs = [pl.BlockSpec((1, rows, d_inner), lambda i, j: (i, j, 0)),
                pl.BlockSpec((1, rows, d_inner), lambda i, j: (i, j, 1)),
                pl.BlockSpec((1, rows, gn), lambda i, j: (i, j, 2 * d_inner // gn)),
                pl.BlockSpec((1, rows, gn), lambda i, j: (i, j, 2 * d_inner // gn + 1)),
                pl.BlockSpec((1, rows, LANES), lambda i, j: (i, j, 0)),
                ] + const_specs
    args = [zx, zx, zx, zx, dt_raw] + consts
    if not zero_init:
        in_specs += [pl.BlockSpec((1, SSD_CONV - 1, conv_dim), lambda i, j: (i, 0, 0)),
                     pl.BlockSpec((1, n_heads, SSD_HEADDIM, SSD_STATE), lambda i, j: (i, 0, 0, 0))]
        args += [conv_state, ssm_state]
    out_shape = (jax.ShapeDtypeStruct((b, s, d_inner), BF16),
                 jax.ShapeDtypeStruct((b, SSD_CONV - 1, conv_dim), F32),
                 jax.ShapeDtypeStruct((b, n_heads, SSD_HEADDIM, SSD_STATE), F32))
    out_specs = (pl.BlockSpec((1, rows, d_inner), lambda i, j: (i, j, 0)),
                 pl.BlockSpec((1, SSD_CONV - 1, conv_dim), lambda i, j: (i, 0, 0)),
                 pl.BlockSpec((1, n_heads, SSD_HEADDIM, SSD_STATE), lambda i, j: (i, 0, 0, 0)))
    body = functools.partial(_ssd_body, chunks=chunks, valid=valid, zero_init=zero_init,
                             n_steps=n_chunks // chunks, n_heads=n_heads)
    return pl.pallas_call(
        body,
        grid=(b, n_chunks // chunks),
        in_specs=in_specs,
        out_specs=out_specs,
        out_shape=out_shape,
        scratch_shapes=[pltpu.VMEM((SUBLANES, conv_dim), F32),
                        pltpu.VMEM((chunks, L, conv_dim), F32),
                        pltpu.VMEM((SSD_GROUPS, SSD_STATE, d_inner // SSD_GROUPS), F32)],
        compiler_params=_params("arbitrary", "arbitrary"),
        name="ssd_scan",
    )(*args)


def _qkv_body(*refs, scale, layout, aliased):
    x_ref, g_ref, w_ref, gq_ref, gk_ref, avg_ref = refs[:6]
    n_in = 6 + (1 if layout else 0) + (2 if aliased else 0)
    qo_ref, ko_ref, vo_ref = refs[n_in:n_in + 3]
    d = x_ref.shape[1]
    xn = _rms_scale(x_ref[...], g_ref[...]).astype(BF16)

    def head_norm(x, g):
        slab = avg_ref.shape[0]
        sq = (x * x).astype(BF16)
        ms = jnp.concatenate([_dot_f32(sq[:, c:c + slab], avg_ref[...])
                              for c in range(0, x.shape[1], slab)], axis=1)
        return x * lax.rsqrt(ms + EPS) * g

    qo_ref[...] = (head_norm(_dot_f32(xn, w_ref[:, 0:d]), gq_ref[...]) * scale).astype(qo_ref.dtype)
    k = head_norm(_dot_f32(xn, w_ref[:, d:2 * d]), gk_ref[...])
    v = _dot_f32(xn, w_ref[:, 2 * d:3 * d])
    heads = d // SB_HEADDIM
    ko_ref[0] = k.reshape(k.shape[0], heads, SB_HEADDIM)
    vo_ref[0] = v.reshape(v.shape[0], heads, SB_HEADDIM)
    if layout:
        _kv_layout_rows(k, v, refs[6][...], refs[n_in + 3], refs[n_in + 4])


def _kv_layout_rows(k, v, perm, kp_ref, vt_ref):
    pairs = kp_ref.shape[1]
    for blk in range(k.shape[0] // KEY_BLOCK):
        rows = slice(blk * KEY_BLOCK, (blk + 1) * KEY_BLOCK)
        kp = _dot_f32(perm, k[rows].astype(BF16))
        vp = _dot_f32(perm, v[rows].astype(BF16))
        for p in range(pairs):
            lanes = slice(p * LANES, (p + 1) * LANES)
            kp_ref[0, p, blk] = kp[:, lanes].astype(BF16)
            vt_ref[0, p, blk] = vp[:, lanes].T.astype(BF16)


def _key_permutation():
    r = jnp.arange(KEY_BLOCK)
    key_of_row = (r % SUBLANES) * VREGS_PER_KEY_BLOCK + r // SUBLANES
    return (key_of_row[:, None] == r[None, :]).astype(BF16)


def qkv_project(x, g, w, q_gain, k_gain, tm, *, slot, n_slots, stacks=None, seq=None):
    t, d = x.shape
    n_heads = d // SB_HEADDIM
    head_of = jnp.arange(MXU_WIDTH) // SB_HEADDIM
    avg = jnp.where(head_of[:, None] == head_of[None, :], 1.0 / SB_HEADDIM, 0.0).astype(BF16)
    gq = jnp.tile(q_gain, n_heads).reshape(1, d)
    gk = jnp.tile(k_gain, n_heads).reshape(1, d)
    layout = seq is not None
    aliased = stacks is not None
    body = functools.partial(_qkv_body, scale=0.5 * SB_HEADDIM ** -0.5, layout=layout, aliased=aliased)
    row_spec = pl.BlockSpec((tm, d), lambda i: (i, 0))
    vec_spec = pl.BlockSpec((1, d), lambda i: (0, 0))
    stack_spec = pl.BlockSpec((1, tm, n_heads, SB_HEADDIM), lambda i: (slot, i, 0, 0))
    stack_shape = jax.ShapeDtypeStruct((n_slots, t, n_heads, SB_HEADDIM), F32)
    in_specs = [row_spec, vec_spec, pl.BlockSpec((d, 3 * d), lambda i: (0, 0)), vec_spec, vec_spec,
                pl.BlockSpec((MXU_WIDTH, MXU_WIDTH), lambda i: (0, 0))]
    args = [x, g.reshape(1, d), w, gq, gk, avg]
    out_specs = [row_spec, stack_spec, stack_spec]
    out_shape = [jax.ShapeDtypeStruct((t, d), BF16), stack_shape, stack_shape]
    if layout:
        assert tm % KEY_BLOCK == 0 and seq % tm == 0
        tiles, blocks, pairs = seq // tm, tm // KEY_BLOCK, d // LANES
        in_specs.append(pl.BlockSpec((KEY_BLOCK, KEY_BLOCK), lambda i: (0, 0)))
        args.append(_key_permutation())
        lay_spec = pl.BlockSpec((1, pairs, blocks, KEY_BLOCK, LANES),
                                lambda i: (i // tiles, 0, i % tiles, 0, 0))
        lay_shape = jax.ShapeDtypeStruct((t // seq, pairs, seq // KEY_BLOCK, KEY_BLOCK, LANES), BF16)
        out_specs += [lay_spec, lay_spec]
        out_shape += [lay_shape, lay_shape]
    aliases = {}
    if aliased:
        aliases = {len(args): 1, len(args) + 1: 2}
        in_specs += [pl.BlockSpec(memory_space=pl.ANY)] * 2
        args += list(stacks)
    return pl.pallas_call(
        body,
        grid=(t // tm,),
        in_specs=in_specs,
        out_specs=tuple(out_specs),
        out_shape=tuple(out_shape),
        input_output_aliases=aliases,
        compiler_params=_params("arbitrary"),
        name="qkv_project",
    )(*args)


def _kv_layout_body(k_ref, v_ref, perm_ref, kp_ref, vt_ref):
    _kv_layout_rows(k_ref[0], v_ref[0], perm_ref[...], kp_ref, vt_ref)


def kv_layout(k, v, b, batch0=0):
    _, s, d = k.shape
    nkb = s // KEY_BLOCK
    blocks = 8 if nkb % 8 == 0 else nkb
    pairs = d // LANES
    out = jax.ShapeDtypeStruct((b, pairs, nkb, KEY_BLOCK, LANES), BF16)
    out_spec = pl.BlockSpec((1, pairs, blocks, KEY_BLOCK, LANES), lambda i, j: (i, 0, j, 0, 0))
    return pl.pallas_call(
        _kv_layout_body,
        grid=(b, nkb // blocks),
        in_specs=[pl.BlockSpec((1, blocks * KEY_BLOCK, d), lambda i, j: (i + batch0, j, 0)),
                  pl.BlockSpec((1, blocks * KEY_BLOCK, d), lambda i, j: (i + batch0, j, 0)),
                  pl.BlockSpec((KEY_BLOCK, KEY_BLOCK), lambda i, j: (0, 0))],
        out_specs=(out_spec, out_spec),
        out_shape=(out, out),
        compiler_params=_params("arbitrary", "arbitrary"),
        name="kv_layout",
    )(k, v, _key_permutation())


def _attn_tile(qi, q_ref, k_ref, vt_ref, o_ref, zbuf_ref, wbuf_ref, *, tq, q_pos0):
    nv = VREGS_PER_KEY_BLOCK
    q = q_ref[0, qi * tq:(qi + 1) * tq, :]
    lane = lax.broadcasted_iota(jnp.int32, (tq, LANES), 1)
    qms = [jnp.where(lane // SB_HEADDIM == hh, q, jnp.zeros_like(q)) for hh in range(2)]
    r_io = lax.broadcasted_iota(jnp.int32, (KEY_BLOCK, KEY_BLOCK), 0)
    diag_mask = ((r_io % SUBLANES) * nv + r_io // SUBLANES
                 < lax.broadcasted_iota(jnp.int32, (KEY_BLOCK, KEY_BLOCK), 1))
    n_diag = tq // KEY_BLOCK
    n_clear = (q_pos0 + qi * tq) // KEY_BLOCK

    def shift_up(a, k):
        sub = lax.broadcasted_iota(jnp.int32, a.shape, 0)
        return jnp.where(sub < SUBLANES - k, pltpu.roll(a, SUBLANES - k, 0), 1.0)

    def scores(j0, n, hh):
        kb = k_ref[0, 0, pl.ds(j0, n)].reshape(n * KEY_BLOCK, LANES)
        return lax.dot_general(kb, qms[hh], (((1,), (1,)), ((), ())), preferred_element_type=F32)

    def values(j0, n, hh, w):
        last = k_ref.shape[2] - 1
        vt = jnp.concatenate([vt_ref[0, 0, jnp.minimum(j0 + r, last)][hh * SB_HEADDIM:(hh + 1) * SB_HEADDIM, :]
                              for r in range(n)], axis=1)
        return _dot_f32(vt, w)

    def weights(z, n, rest, diag):
        ws = [None] * n
        for r in range(n - 1, -1, -1):
            lo = r * KEY_BLOCK if diag else 0
            width = tq - lo
            om = 0.5 - 0.5 * jnp.tanh(z[r * KEY_BLOCK:(r + 1) * KEY_BLOCK, lo:])
            if diag:
                om_d = jnp.where(diag_mask, om[:, :KEY_BLOCK], 1.0)
                om = jnp.concatenate([om_d, om[:, KEY_BLOCK:]], axis=1) if width > KEY_BLOCK else om_d
            om = om.reshape(nv, SUBLANES, width)
            suffix = [None] * (nv + 1)
            suffix[nv] = 1.0
            suffix[nv - 1] = om[nv - 1]
            for v in range(nv - 2, -1, -1):
                suffix[v] = om[v] * suffix[v + 1]
            row_tot = suffix[0]
            later = shift_up(row_tot, 1)
            later = later * shift_up(later, 1)
            later = later * shift_up(later, 2)
            later = later * shift_up(later, 4)
            rest_r = rest[:, lo:]
            f = later * rest_r
            w = jnp.stack([(suffix[v + 1] - suffix[v]) * f for v in range(nv)],
                          axis=0).reshape(KEY_BLOCK, width).astype(BF16)
            tot = jnp.broadcast_to((row_tot * later)[0:1, :], (SUBLANES, width))
            if lo:
                w = jnp.concatenate([jnp.zeros((KEY_BLOCK, lo), BF16), w], axis=1)
                rest = jnp.concatenate([rest[:, :lo], rest_r * tot], axis=1)
            else:
                rest = rest_r * tot
            ws[r] = w
        return jnp.concatenate(ws, axis=0), rest

    nb = KEY_BLOCKS_PER_STEP
    n_far = n_clear // nb

    def group_start(t):
        return n_clear - nb * t

    ones = jnp.ones((SUBLANES, tq), F32)
    z0 = scores(n_clear, n_diag, 0)
    z1 = scores(n_clear, n_diag, 1)
    zbuf_ref[...] = scores(jnp.maximum(group_start(1), 0), nb, 0)
    w0, rest0 = weights(z0, n_diag, ones, True)
    acc0 = values(n_clear, n_diag, 0, w0)
    w1, rest1 = weights(z1, n_diag, ones, True)
    acc1 = jnp.zeros((SB_HEADDIM, tq), F32)
    if n_diag > nb:
        acc1 = values(n_clear, n_diag, 1, w1)
        wbuf_ref[...] = jnp.zeros(wbuf_ref.shape, BF16)
    else:
        if n_diag < nb:
            wbuf_ref[n_diag * KEY_BLOCK:, :] = jnp.zeros(((nb - n_diag) * KEY_BLOCK, tq), BF16)
        wbuf_ref[0:n_diag * KEY_BLOCK, :] = w1

    def far(t, carry):
        rest0, acc0, rest1, acc1 = carry
        j0 = group_start(t)
        acc1 = acc1 + values(j0 + nb, nb, 1, wbuf_ref[...])
        z1 = scores(j0, nb, 1)
        w0, rest0 = weights(zbuf_ref[...], nb, rest0, False)
        acc0 = acc0 + values(j0, nb, 0, w0)
        w1, rest1 = weights(z1, nb, rest1, False)
        wbuf_ref[...] = w1
        zbuf_ref[...] = scores(jnp.maximum(j0 - nb, 0), nb, 0)
        return rest0, acc0, rest1, acc1

    _, acc0, _, acc1 = lax.fori_loop(1, n_far + 1, far, (rest0, acc0, rest1, acc1), unroll=True)
    acc1 = acc1 + values(group_start(n_far), nb, 1, wbuf_ref[...])
    o_ref[0, qi * tq:(qi + 1) * tq, :] = jnp.concatenate([acc0, acc1], axis=0).T.astype(o_ref.dtype)


def _attn_body(q_ref, k_ref, vt_ref, o_ref, zbuf_ref, wbuf_ref, *, tq, q_pos0):
    for qi in range(q_ref.shape[1] // tq):
        _attn_tile(qi, q_ref, k_ref, vt_ref, o_ref, zbuf_ref, wbuf_ref, tq=tq, q_pos0=q_pos0)


def sb_attention(q, kp, vt, *, tq, q_pos0):
    b, sq, d = q.shape
    pairs, nkb = kp.shape[1], kp.shape[2]
    n_tiles = sq // tq
    for t in range(n_tiles):
        assert (q_pos0 + t * tq) % (KEY_BLOCK * KEY_BLOCKS_PER_STEP) == 0
    assert q_pos0 + sq <= nkb * KEY_BLOCK
    body = functools.partial(_attn_body, tq=tq, q_pos0=q_pos0)
    return pl.pallas_call(
        body,
        grid=(b, pairs),
        in_specs=[pl.BlockSpec((1, sq, LANES), lambda i, p: (i, 0, p)),
                  pl.BlockSpec((1, 1, nkb, KEY_BLOCK, LANES), lambda i, p: (i, p, 0, 0, 0)),
                  pl.BlockSpec((1, 1, nkb, LANES, KEY_BLOCK), lambda i, p: (i, p, 0, 0, 0))],
        out_specs=pl.BlockSpec((1, sq, LANES), lambda i, p: (i, 0, p)),
        out_shape=jax.ShapeDtypeStruct((b, sq, d), BF16),
        scratch_shapes=[pltpu.VMEM((KEY_BLOCKS_PER_STEP * KEY_BLOCK, tq), F32),
                        pltpu.VMEM((KEY_BLOCKS_PER_STEP * KEY_BLOCK, tq), BF16)],
        compiler_params=_params("arbitrary", "arbitrary"),
        name="sb_attention",
    )(q, kp, vt)


def _row_tile(t):
    return 1024 if t % 1024 == 0 else t


def _ssd_layer(y, b, s, conv_state, ssm_state, g_mix, in_w, conv_w, conv_b, dt_bias, a_log, d_skip,
               norm_w, out_w):
    t, d = y.shape
    n_heads = dt_bias.shape[0]
    n_main = in_w.shape[1] - n_heads
    w_dt = jnp.pad(in_w[:, n_main:], ((0, 0), (0, LANES - n_heads))).astype(BF16)
    zx, dt_raw = norm_matmul(y, g_mix, in_w[:, :n_main].astype(BF16), _row_tile(t), 2048,
                             out_dtype=BF16, w_side=w_dt)
    zx = zx.reshape(b, s, n_main)
    dt_raw = dt_raw.reshape(b, s, LANES)
    valid = SSD_CHUNK
    if s % SSD_CHUNK:
        assert s < SSD_CHUNK
        valid = s
        zx = jnp.pad(zx, ((0, 0), (0, SSD_CHUNK - s), (0, 0)))
        dt_raw = jnp.pad(dt_raw, ((0, 0), (0, SSD_CHUNK - s), (0, 0)))
    y_ssd, conv_out, ssm_out = ssd_scan(zx, dt_raw, conv_state, ssm_state, conv_w, conv_b, dt_bias,
                                        a_log, d_skip, norm_w, valid=valid)
    y_ssd = y_ssd[:, :s].reshape(t, -1)
    return matmul_res(y_ssd, out_w.astype(BF16), y, _row_tile(t)), conv_out, ssm_out


def _sb_layer(y, b, s, cache_k, cache_v, slot, n_slots, stacks, g_mix, qkv_w, q_gain, k_gain, out_w):
    t, d = y.shape
    tm = _row_tile(t)
    w = qkv_w.astype(BF16)
    if cache_k is None:
        tq, q_pos0 = KEY_BLOCKS_PER_STEP * KEY_BLOCK, 0
        q, ks, vs, kp, vt = qkv_project(y, g_mix, w, q_gain, k_gain, min(tm, 512), slot=slot,
                                        n_slots=n_slots, stacks=stacks, seq=s)
        q = q.reshape(b, s, d)
    else:
        past = cache_k.shape[1]
        tq, q_pos0 = KEY_BLOCK, past
        q, ks, vs = qkv_project(y, g_mix, w, q_gain, k_gain, min(tm, 512), slot=slot,
                                n_slots=n_slots, stacks=stacks)
        pad = ((0, 0), (0, tq - s), (0, 0))
        kp_c, vt_c = kv_layout(cache_k, cache_v, b, batch0=slot * b)
        kp_n, vt_n = kv_layout(jnp.pad(ks[slot].reshape(b, s, d), pad),
                               jnp.pad(vs[slot].reshape(b, s, d), pad), b)
        kp = jnp.concatenate([kp_c, kp_n], axis=2)
        vt = jnp.concatenate([vt_c, vt_n], axis=2)
        q = jnp.pad(q.reshape(b, s, d), pad)
    o = sb_attention(q, kp, vt, tq=tq, q_pos0=q_pos0)[:, :s].reshape(t, d)
    return matmul_res(o, out_w.astype(BF16), y, tm), (ks, vs)


def kernel(x_prompt, x_sample, state_ssm, state_conv, cache_k, cache_v, norm_mix, norm_mlp, ssd_in_w, ssd_conv_w, ssd_conv_b, ssd_dt_bias, ssd_a_log, ssd_d, ssd_norm_w, ssd_out_w, sb_qkv_w, sb_q_gain, sb_k_gain, sb_out_w, mlp_up, mlp_down):
    bp, sp, d = x_prompt.shape
    bs, ss, _ = x_sample.shape
    depth = norm_mix.shape[0]
    y_p = x_prompt.reshape(bp * sp, d)
    y_s = x_sample.reshape(bs * ss, d)
    n_sb = sb_qkv_w.shape[0]
    heads = d // SB_HEADDIM
    p_ssm, p_conv, s_ssm, s_conv = [], [], [], []
    p_kv = s_kv = None
    past = cache_k.shape[2]
    cache_k2 = cache_k.reshape(n_sb * bs, past, d)
    cache_v2 = cache_v.reshape(n_sb * bs, past, d)
    for i in range(depth):
        j = i // 2
        if i % 2 == 0:
            prm = (norm_mix[i], ssd_in_w[j], ssd_conv_w[j], ssd_conv_b[j], ssd_dt_bias[j],
                   ssd_a_log[j], ssd_d[j], ssd_norm_w[j], ssd_out_w[j])
            y_p, c_p, st_p = _ssd_layer(y_p, bp, sp, None, None, *prm)
            y_s, c_s, st_s = _ssd_layer(y_s, bs, ss, state_conv[j], state_ssm[j], *prm)
            p_ssm.append(st_p); p_conv.append(c_p)
            s_ssm.append(st_s); s_conv.append(c_s)
        else:
            prm = (norm_mix[i], sb_qkv_w[j], sb_q_gain[j], sb_k_gain[j], sb_out_w[j])
            y_p, p_kv = _sb_layer(y_p, bp, sp, None, None, j, n_sb, p_kv, *prm)
            y_s, s_kv = _sb_layer(y_s, bs, ss, cache_k2, cache_v2, j, n_sb, s_kv, *prm)
        wu = mlp_up[i].astype(BF16)
        wd = mlp_down[i].astype(BF16)
        y_p = mlp(y_p, norm_mlp[i], wu, wd, _row_tile(y_p.shape[0]), 1024)
        y_s = mlp(y_s, norm_mlp[i], wu, wd, _row_tile(y_s.shape[0]), 1024)
    p_k, p_v = (a.reshape(n_sb, bp, sp, heads, SB_HEADDIM) for a in p_kv)
    s_k, s_v = (a.reshape(n_sb, bs, ss, heads, SB_HEADDIM) for a in s_kv)
    return (y_p.reshape(bp, sp, d), y_s.reshape(bs, ss, d),
            jnp.stack(p_ssm), jnp.stack(p_conv), p_k, p_v,
            jnp.stack(s_ssm), jnp.stack(s_conv), s_k, s_v)
```

```python
import functools

import jax
import jax.numpy as jnp
from jax import lax
from jax.experimental import pallas as pl
from jax.experimental.pallas import tpu as pltpu

F32 = jnp.float32
BF16 = jnp.bfloat16
EPS = 1e-6

LANES = 128
SUBLANES = 8
MXU_WIDTH = 256
VMEM_LIMIT_BYTES = 56 * 1024 * 1024

SSD_HEADDIM = 64
SSD_GROUPS = 8
SSD_STATE = 128
SSD_CONV = 4
SB_HEADDIM = 64
SSD_CHUNK = 128
SSD_CHUNKS_PER_STEP = 4
KEY_BLOCK = 128
VREGS_PER_KEY_BLOCK = KEY_BLOCK // SUBLANES
KEY_BLOCKS_PER_STEP = 4


def _params(*sem):
    return pltpu.CompilerParams(dimension_semantics=sem, vmem_limit_bytes=VMEM_LIMIT_BYTES)


def _rms_scale(x, g):
    ms = jnp.mean(x * x, axis=-1, keepdims=True)
    return x * lax.rsqrt(ms + EPS) * g


def _silu(x):
    h = 0.5 * x
    return h + h * jnp.tanh(h)


def _softplus(x):
    return jnp.maximum(x, 0.0) + jnp.log(1.0 + jnp.exp(-jnp.abs(x)))


def _split3(a):
    hi = a.astype(BF16)
    rem = a - hi.astype(F32)
    mid = rem.astype(BF16)
    lo = (rem - mid.astype(F32)).astype(BF16)
    return hi, mid, lo


def _dot_f32(a, b):
    return jnp.dot(a, b, preferred_element_type=F32)


def _norm_matmul_body(x_ref, g_ref, w_ref, *rest):
    side = len(rest) == 4
    if side:
        ws_ref, o_ref, os_ref, xn_ref = rest
    else:
        o_ref, xn_ref = rest

    @pl.when(pl.program_id(1) == 0)
    def _():
        xn_ref[...] = _rms_scale(x_ref[...], g_ref[...]).astype(BF16)
        if side:
            os_ref[...] = _dot_f32(xn_ref[...], ws_ref[...])

    o_ref[...] = _dot_f32(xn_ref[...], w_ref[...]).astype(o_ref.dtype)


def norm_matmul(x, g, w, tm, tn, out_dtype=F32, w_side=None):
    t, d = x.shape
    n = w.shape[1]
    in_specs = [pl.BlockSpec((tm, d), lambda i, j: (i, 0)),
                pl.BlockSpec((1, d), lambda i, j: (0, 0)),
                pl.BlockSpec((d, tn), lambda i, j: (0, j))]
    out_specs = pl.BlockSpec((tm, tn), lambda i, j: (i, j))
    out_shape = jax.ShapeDtypeStruct((t, n), out_dtype)
    args = [x, g.reshape(1, d), w]
    if w_side is not None:
        in_specs.append(pl.BlockSpec((d, LANES), lambda i, j: (0, 0)))
        out_specs = (out_specs, pl.BlockSpec((tm, LANES), lambda i, j: (i, 0)))
        out_shape = (out_shape, jax.ShapeDtypeStruct((t, LANES), F32))
        args.append(w_side)
    return pl.pallas_call(
        _norm_matmul_body,
        grid=(t // tm, n // tn),
        in_specs=in_specs,
        out_specs=out_specs,
        out_shape=out_shape,
        scratch_shapes=[pltpu.VMEM((tm, d), BF16)],
        compiler_params=_params("arbitrary", "arbitrary"),
        name="norm_matmul",
    )(*args)


def _matmul_res_body(a_ref, w_ref, r_ref, o_ref):
    o_ref[...] = r_ref[...] + _dot_f32(a_ref[...].astype(BF16), w_ref[...])


def matmul_res(a, w, res, tm):
    t, k = a.shape
    n = w.shape[1]
    return pl.pallas_call(
        _matmul_res_body,
        grid=(t // tm,),
        in_specs=[pl.BlockSpec((tm, k), lambda i: (i, 0)),
                  pl.BlockSpec((k, n), lambda i: (0, 0)),
                  pl.BlockSpec((tm, n), lambda i: (i, 0))],
        out_specs=pl.BlockSpec((tm, n), lambda i: (i, 0)),
        out_shape=jax.ShapeDtypeStruct((t, n), F32),
        compiler_params=_params("arbitrary"),
        name="matmul_res",
    )(a, w, res)


def _mlp_body(x_ref, g_ref, wu_ref, wd_ref, o_ref, xn_ref):
    @pl.when(pl.program_id(1) == 0)
    def _():
        x = x_ref[...]
        xn_ref[...] = _rms_scale(x, g_ref[...]).astype(BF16)
        o_ref[...] = x

    h = jnp.maximum(_dot_f32(xn_ref[...], wu_ref[...]), 0.0)
    o_ref[...] += _dot_f32((h * h).astype(BF16), wd_ref[...])


def mlp(x, g, wu, wd, tm, tf):
    t, d = x.shape
    f = wu.shape[1]
    return pl.pallas_call(
        _mlp_body,
        grid=(t // tm, f // tf),
        in_specs=[pl.BlockSpec((tm, d), lambda i, j: (i, 0)),
                  pl.BlockSpec((1, d), lambda i, j: (0, 0)),
                  pl.BlockSpec((d, tf), lambda i, j: (0, j)),
                  pl.BlockSpec((tf, d), lambda i, j: (j, 0))],
        out_specs=pl.BlockSpec((tm, d), lambda i, j: (i, 0)),
        out_shape=jax.ShapeDtypeStruct((t, d), F32),
        scratch_shapes=[pltpu.VMEM((tm, d), BF16)],
        compiler_params=_params("arbitrary", "arbitrary"),
        name="mlp",
    )(x, g.reshape(1, d), wu, wd)


def _ssd_body(*refs, chunks, **static):
    for cc in range(chunks):
        _ssd_chunk(*refs, row0=cc * SSD_CHUNK, last=cc == chunks - 1, **static)


def _ssd_chunk(*refs, row0, last, valid, zero_init, n_steps, n_heads):
    if zero_init:
        (z_ref, x_ref, b_ref, c_ref, dt_ref, cw_ref, cb_ref, dtb_ref, alog_ref, dsk_ref, nw_ref,
         hx_ref, sh_ref, y_ref, cso_ref, sso_ref, tail_ref, act_ref, ht_ref) = refs
        cs_ref = ss_ref = None
    else:
        (z_ref, x_ref, b_ref, c_ref, dt_ref, cw_ref, cb_ref, dtb_ref, alog_ref, dsk_ref, nw_ref,
         hx_ref, sh_ref, cs_ref, ss_ref, y_ref, cso_ref, sso_ref, tail_ref, act_ref, ht_ref) = refs
    L = SSD_CHUNK
    G = SSD_GROUPS
    R = n_heads // G
    P = SSD_HEADDIM
    GP = R * P
    d_inner = n_heads * P
    gn = G * SSD_STATE
    conv_dim = d_inner + 2 * gn
    rows = slice(row0, row0 + L)
    z_ref, x_ref, b_ref, c_ref, dt_ref, y_ref = (
        r.at[:, rows] for r in (z_ref, x_ref, b_ref, c_ref, dt_ref, y_ref))
    act_ref = act_ref.at[row0 // L]
    step = pl.program_id(1)

    @pl.when((step == 0) & (row0 == 0))
    def _init():
        tail_ref[...] = jnp.zeros(tail_ref.shape, F32)
        if zero_init:
            ht_ref[...] = jnp.zeros(ht_ref.shape, F32)
        else:
            tail_ref[SUBLANES - (SSD_CONV - 1):SUBLANES, :] = cs_ref[0]
            for g in range(G):
                ht_ref[g] = ss_ref[0, R * g:R * (g + 1)].reshape(GP, SSD_STATE).T

    col_chunk = 512
    for j in range(conv_dim // col_chunk):
        cols = slice(j * col_chunk, (j + 1) * col_chunk)
        if cols.stop <= d_inner:
            xb = x_ref[0, :, cols]
        elif cols.stop <= d_inner + gn:
            xb = b_ref[0, :, cols.start - d_inner:cols.stop - d_inner]
        else:
            xb = c_ref[0, :, cols.start - d_inner - gn:cols.stop - d_inner - gn]
        xf = xb.astype(F32)
        edge = jnp.concatenate([tail_ref[:, cols], xf[0:SUBLANES]], axis=0)
        acc = cb_ref[:, cols] + cw_ref[SSD_CONV - 1:SSD_CONV, cols] * xf
        for k in range(1, SSD_CONV):
            delayed = jnp.concatenate([pltpu.roll(edge, k, 0)[SUBLANES:],
                                       _dot_f32(sh_ref[k - 1], xb)[SUBLANES:]], axis=0)
            acc = acc + cw_ref[SSD_CONV - 1 - k:SSD_CONV - k, cols] * delayed
        act_ref[:, cols] = _silu(acc)
        tail_ref[:, cols] = xf[L - SUBLANES:]
        cso_ref[0, :, cols] = xf[valid - (SSD_CONV - 1):valid]

    dt = _softplus(dt_ref[0] + dtb_ref[...])
    t_io = lax.broadcasted_iota(jnp.int32, (L, L), 0)
    s_io = lax.broadcasted_iota(jnp.int32, (L, L), 1)
    causal = t_io >= s_io
    if valid < L:
        row = lax.broadcasted_iota(jnp.int32, (L, LANES), 0)
        dt = jnp.where(row < valid, dt, 0.0)
    adt = dt * (-jnp.exp(alog_ref[...]))
    tri = jnp.where(causal, 1.0, 0.0).astype(BF16)
    acs = sum(_dot_f32(tri, p) for p in _split3(adt))
    hx = hx_ref[...]
    e_x = jnp.exp(sum(_dot_f32(p, hx) for p in _split3(acs)))
    alast = acs[L - 1:L, :]
    acs_t = acs.T
    src_t = (acs - jnp.log(dt)).T
    u_t = (dt * jnp.exp(alast - acs)).T
    head_of_lane = (lax.broadcasted_iota(jnp.int32, (L, GP), 1) // P).astype(F32).astype(BF16)

    for g in range(G):
        gc = slice(g * GP, (g + 1) * GP)
        bg = act_ref[:, d_inner + g * SSD_STATE:d_inner + (g + 1) * SSD_STATE]
        cg = act_ref[:, d_inner + gn + g * SSD_STATE:d_inner + gn + (g + 1) * SSD_STATE]
        cbf = cg.astype(BF16)
        cb = lax.dot_general(cbf, bg.astype(BF16), (((1,), (1,)), ((), ())),
                             preferred_element_type=F32)
        bg_t = bg.T
        h_t = ht_ref[g]
        y_off = _dot_f32(cbf, h_t.astype(BF16)) * e_x[:, gc]
        xg = act_ref[:, gc]
        xb = xg.astype(BF16)
        x_bd = jnp.concatenate([jnp.where(head_of_lane == r, xb, jnp.zeros_like(xb))
                                for r in range(R)], axis=0)
        ms, bs = [], []
        for r in range(R):
            h = g * R + r
            seg = acs[:, h:h + 1] - src_t[h:h + 1, :]
            ms.append((jnp.where(causal, jnp.exp(seg), 0.0) * cb).astype(BF16))
            bs.append((bg_t * u_t[h:h + 1, :]).astype(BF16))
        y_d = _dot_f32(jnp.concatenate(ms, axis=1), x_bd)
        upd = _dot_f32(jnp.concatenate(bs, axis=1), x_bd)
        yg = (y_d + y_off + dsk_ref[:, gc] * xg) * _silu(z_ref[0, :, gc].astype(F32))
        y_ref[0, :, gc] = _rms_scale(yg, nw_ref[:, gc]).astype(y_ref.dtype)
        ht_ref[g] = h_t * e_x[L - 1:L, gc] + upd

    @pl.when((step == n_steps - 1) & last)
    def _state_out():
        for g in range(G):
            sso_ref[0, R * g:R * (g + 1)] = ht_ref[g].T.reshape(R, P, SSD_STATE)


def ssd_scan(zx, dt_raw, conv_state, ssm_state, conv_w, conv_b, dt_bias, a_log, d_skip, norm_w, *, valid):
    b, s, _ = zx.shape
    n_heads = dt_bias.shape[0]
    d_inner = n_heads * SSD_HEADDIM
    gn = SSD_GROUPS * SSD_STATE
    conv_dim = d_inner + 2 * gn
    L = SSD_CHUNK
    n_chunks = s // L
    chunks = SSD_CHUNKS_PER_STEP if n_chunks % SSD_CHUNKS_PER_STEP == 0 else 1
    rows = chunks * L
    zero_init = conv_state is None
    pad_h = LANES - n_heads
    head_expand = (jnp.arange(LANES)[:, None] == jnp.arange(d_inner)[None, :] // SSD_HEADDIM)
    consts = [conv_w, conv_b.reshape(1, conv_dim),
              jnp.pad(dt_bias, (0, pad_h)).reshape(1, LANES),
              jnp.pad(a_log, (0, pad_h)).reshape(1, LANES),
              jnp.repeat(d_skip, SSD_HEADDIM).reshape(1, d_inner),
              norm_w.reshape(1, d_inner),
              head_expand.astype(BF16)]
    const_specs = [pl.BlockSpec(a.shape, lambda i, j: (0, 0)) for a in consts]
    t_io = jnp.arange(L)
    shifts = jnp.stack([t_io[:, None] - k == t_io[None, :] for k in range(1, SSD_CONV)]).astype(BF16)
    consts.append(shifts)
    const_specs.append(pl.BlockSpec(shifts.shape, lambda i, j: (0, 0, 0)))
    in_specs = [pl.BlockSpec((1, rows, d_inner), lambda i, j: (i, j, 0)),
                pl.BlockSpec((1, rows, d_inner), lambda i, j: (i, j, 1)),
                pl.BlockSpec((1, rows, gn), lambda i, j: (i, j, 2 * d_inner // gn)),
                pl.BlockSpec((1, rows, gn), lambda i, j: (i, j, 2 * d_inner // gn + 1)),
                pl.BlockSpec((1, rows, LANES), lambda i, j: (i, j, 0)),
                ] + const_specs
    args = [zx, zx, zx, zx, dt_raw] + consts
    if not zero_init:
        in_specs += [pl.BlockSpec((1, SSD_CONV - 1, conv_dim), lambda i, j: (i, 0, 0)),
                     pl.BlockSpec((1, n_heads, SSD_HEADDIM, SSD_STATE), lambda i, j: (i, 0, 0, 0))]
        args += [conv_state, ssm_state]
    out_shape = (jax.ShapeDtypeStruct((b, s, d_inner), BF16),
                 jax.ShapeDtypeStruct((b, SSD_CONV - 1, conv_dim), F32),
                 jax.ShapeDtypeStruct((b, n_heads, SSD_HEADDIM, SSD_STATE), F32))
    out_specs = (pl.BlockSpec((1, rows, d_inner), lambda i, j: (i, j, 0)),
                 pl.BlockSpec((1, SSD_CONV - 1, conv_dim), lambda i, j: (i, 0, 0)),
                 pl.BlockSpec((1, n_heads, SSD_HEADDIM, SSD_STATE), lambda i, j: (i, 0, 0, 0)))
    body = functools.partial(_ssd_body, chunks=chunks, valid=valid, zero_init=zero_init,
                             n_steps=n_chunks // chunks, n_heads=n_heads)
    return pl.pallas_call(
        body,
        grid=(b, n_chunks // chunks),
        in_specs=in_specs,
        out_specs=out_specs,
        out_shape=out_shape,
        scratch_shapes=[pltpu.VMEM((SUBLANES, conv_dim), F32),
                        pltpu.VMEM((chunks, L, conv_dim), F32),
                        pltpu.VMEM((SSD_GROUPS, SSD_STATE, d_inner // SSD_GROUPS), F32)],
        compiler_params=_params("arbitrary", "arbitrary"),
        name="ssd_scan",
    )(*args)


def _qkv_body(*refs, scale, layout, aliased):
    x_ref, g_ref, w_ref, gq_ref, gk_ref, avg_ref = refs[:6]
    n_in = 6 + (1 if layout else 0) + (2 if aliased else 0)
    qo_ref, ko_ref, vo_ref = refs[n_in:n_in + 3]
    d = x_ref.shape[1]
    xn = _rms_scale(x_ref[...], g_ref[...]).astype(BF16)

    def head_norm(x, g):
        slab = avg_ref.shape[0]
        sq = (x * x).astype(BF16)
        ms = jnp.concatenate([_dot_f32(sq[:, c:c + slab], avg_ref[...])
                              for c in range(0, x.shape[1], slab)], axis=1)
        return x * lax.rsqrt(ms + EPS) * g

    qo_ref[...] = (head_norm(_dot_f32(xn, w_ref[:, 0:d]), gq_ref[...]) * scale).astype(qo_ref.dtype)
    k = head_norm(_dot_f32(xn, w_ref[:, d:2 * d]), gk_ref[...])
    v = _dot_f32(xn, w_ref[:, 2 * d:3 * d])
    heads = d // SB_HEADDIM
    ko_ref[0] = k.reshape(k.shape[0], heads, SB_HEADDIM)
    vo_ref[0] = v.reshape(v.shape[0], heads, SB_HEADDIM)
    if layout:
        _kv_layout_rows(k, v, refs[6][...], refs[n_in + 3], refs[n_in + 4])


def _kv_layout_rows(k, v, perm, kp_ref, vt_ref):
    pairs = kp_ref.shape[1]
    for blk in range(k.shape[0] // KEY_BLOCK):
        rows = slice(blk * KEY_BLOCK, (blk + 1) * KEY_BLOCK)
        kp = _dot_f32(perm, k[rows].astype(BF16))
        vp = _dot_f32(perm, v[rows].astype(BF16))
        for p in range(pairs):
            lanes = slice(p * LANES, (p + 1) * LANES)
            kp_ref[0, p, blk] = kp[:, lanes].astype(BF16)
            vt_ref[0, p, blk] = vp[:, lanes].T.astype(BF16)


def _key_permutation():
    r = jnp.arange(KEY_BLOCK)
    key_of_row = (r % SUBLANES) * VREGS_PER_KEY_BLOCK + r // SUBLANES
    return (key_of_row[:, None] == r[None, :]).astype(BF16)


def qkv_project(x, g, w, q_gain, k_gain, tm, *, slot, n_slots, stacks=None, seq=None):
    t, d = x.shape
    n_heads = d // SB_HEADDIM
    head_of = jnp.arange(MXU_WIDTH) // SB_HEADDIM
    avg = jnp.where(head_of[:, None] == head_of[None, :], 1.0 / SB_HEADDIM, 0.0).astype(BF16)
    gq = jnp.tile(q_gain, n_heads).reshape(1, d)
    gk = jnp.tile(k_gain, n_heads).reshape(1, d)
    layout = seq is not None
    aliased = stacks is not None
    body = functools.partial(_qkv_body, scale=0.5 * SB_HEADDIM ** -0.5, layout=layout, aliased=aliased)
    row_spec = pl.BlockSpec((tm, d), lambda i: (i, 0))
    vec_spec = pl.BlockSpec((1, d), lambda i: (0, 0))
    stack_spec = pl.BlockSpec((1, tm, n_heads, SB_HEADDIM), lambda i: (slot, i, 0, 0))
    stack_shape = jax.ShapeDtypeStruct((n_slots, t, n_heads, SB_HEADDIM), F32)
    in_specs = [row_spec, vec_spec, pl.BlockSpec((d, 3 * d), lambda i: (0, 0)), vec_spec, vec_spec,
                pl.BlockSpec((MXU_WIDTH, MXU_WIDTH), lambda i: (0, 0))]
    args = [x, g.reshape(1, d), w, gq, gk, avg]
    out_specs = [row_spec, stack_spec, stack_spec]
    out_shape = [jax.ShapeDtypeStruct((t, d), BF16), stack_shape, stack_shape]
    if layout:
        assert tm % KEY_BLOCK == 0 and seq % tm == 0
        tiles, blocks, pairs = seq // tm, tm // KEY_BLOCK, d // LANES
        in_specs.append(pl.BlockSpec((KEY_BLOCK, KEY_BLOCK), lambda i: (0, 0)))
        args.append(_key_permutation())
        lay_spec = pl.BlockSpec((1, pairs, blocks, KEY_BLOCK, LANES),
                                lambda i: (i // tiles, 0, i % tiles, 0, 0))
        lay_shape = jax.ShapeDtypeStruct((t // seq, pairs, seq // KEY_BLOCK, KEY_BLOCK, LANES), BF16)
        out_specs += [lay_spec, lay_spec]
        out_shape += [lay_shape, lay_shape]
    aliases = {}
    if aliased:
        aliases = {len(args): 1, len(args) + 1: 2}
        in_specs += [pl.BlockSpec(memory_space=pl.ANY)] * 2
        args += list(stacks)
    return pl.pallas_call(
        body,
        grid=(t // tm,),
        in_specs=in_specs,
        out_specs=tuple(out_specs),
        out_shape=tuple(out_shape),
        input_output_aliases=aliases,
        compiler_params=_params("arbitrary"),
        name="qkv_project",
    )(*args)


def _kv_layout_body(k_ref, v_ref, perm_ref, kp_ref, vt_ref):
    k, v = k_ref[0], v_ref[0]
    if k.ndim == 3:
        k, v = (a.reshape(a.shape[0], a.shape[1] * a.shape[2]) for a in (k, v))
    _kv_layout_rows(k, v, perm_ref[...], kp_ref, vt_ref)


def kv_layout(k, v, b, batch0=0):
    s = k.shape[1]
    feat = k.shape[2:]
    d = feat[0] if len(feat) == 1 else feat[0] * feat[1]
    nkb = s // KEY_BLOCK
    blocks = 8 if nkb % 8 == 0 else nkb
    pairs = d // LANES
    out = jax.ShapeDtypeStruct((b, pairs, nkb, KEY_BLOCK, LANES), BF16)
    out_spec = pl.BlockSpec((1, pairs, blocks, KEY_BLOCK, LANES), lambda i, j: (i, 0, j, 0, 0))
    row_spec = pl.BlockSpec((1, blocks * KEY_BLOCK) + feat, lambda i, j: (i + batch0, j) + (0,) * len(feat))
    return pl.pallas_call(
        _kv_layout_body,
        grid=(b, nkb // blocks),
        in_specs=[row_spec, row_spec,
                  pl.BlockSpec((KEY_BLOCK, KEY_BLOCK), lambda i, j: (0, 0))],
        out_specs=(out_spec, out_spec),
        out_shape=(out, out),
        compiler_params=_params("arbitrary", "arbitrary"),
        name="kv_layout",
    )(k, v, _key_permutation())


def _attn_tile(qi, q_ref, k_ref, vt_ref, o_ref, zbuf_ref, wbuf_ref, *, tq, q_pos0):
    nv = VREGS_PER_KEY_BLOCK
    q = q_ref[0, qi * tq:(qi + 1) * tq, :]
    lane = lax.broadcasted_iota(jnp.int32, (tq, LANES), 1)
    qms = [jnp.where(lane // SB_HEADDIM == hh, q, jnp.zeros_like(q)) for hh in range(2)]
    r_io = lax.broadcasted_iota(jnp.int32, (KEY_BLOCK, KEY_BLOCK), 0)
    diag_mask = ((r_io % SUBLANES) * nv + r_io // SUBLANES
                 < lax.broadcasted_iota(jnp.int32, (KEY_BLOCK, KEY_BLOCK), 1))
    n_diag = tq // KEY_BLOCK
    n_clear = (q_pos0 + qi * tq) // KEY_BLOCK

    def shift_up(a, k):
        sub = lax.broadcasted_iota(jnp.int32, a.shape, 0)
        return jnp.where(sub < SUBLANES - k, pltpu.roll(a, SUBLANES - k, 0), 1.0)

    def scores(j0, n, hh):
        kb = k_ref[0, 0, pl.ds(j0, n)].reshape(n * KEY_BLOCK, LANES)
        return lax.dot_general(kb, qms[hh], (((1,), (1,)), ((), ())), preferred_element_type=F32)

    def values(j0, n, hh, w):
        last = k_ref.shape[2] - 1
        vt = jnp.concatenate([vt_ref[0, 0, jnp.minimum(j0 + r, last)][hh * SB_HEADDIM:(hh + 1) * SB_HEADDIM, :]
                              for r in range(n)], axis=1)
        return _dot_f32(vt, w)

    def weights(z, n, rest, diag):
        ws = [None] * n
        for r in range(n - 1, -1, -1):
            lo = r * KEY_BLOCK if diag else 0
            width = tq - lo
            om = 0.5 - 0.5 * jnp.tanh(z[r * KEY_BLOCK:(r + 1) * KEY_BLOCK, lo:])
            if diag:
                om_d = jnp.where(diag_mask, om[:, :KEY_BLOCK], 1.0)
                om = jnp.concatenate([om_d, om[:, KEY_BLOCK:]], axis=1) if width > KEY_BLOCK else om_d
            om = om.reshape(nv, SUBLANES, width)
            suffix = [None] * (nv + 1)
            suffix[nv] = 1.0
            suffix[nv - 1] = om[nv - 1]
            for v in range(nv - 2, -1, -1):
                suffix[v] = om[v] * suffix[v + 1]
            row_tot = suffix[0]
            later = shift_up(row_tot, 1)
            later = later * shift_up(later, 1)
            later = later * shift_up(later, 2)
            later = later * shift_up(later, 4)
            rest_r = rest[:, lo:]
            f = later * rest_r
            w = jnp.stack([(suffix[v + 1] - suffix[v]) * f for v in range(nv)],
                          axis=0).reshape(KEY_BLOCK, width).astype(BF16)
            tot = jnp.broadcast_to((row_tot * later)[0:1, :], (SUBLANES, width))
            if lo:
                w = jnp.concatenate([jnp.zeros((KEY_BLOCK, lo), BF16), w], axis=1)
                rest = jnp.concatenate([rest[:, :lo], rest_r * tot], axis=1)
            else:
                rest = rest_r * tot
            ws[r] = w
        return jnp.concatenate(ws, axis=0), rest

    nb = KEY_BLOCKS_PER_STEP
    n_far = n_clear // nb

    def group_start(t):
        return n_clear - nb * t

    ones = jnp.ones((SUBLANES, tq), F32)
    z0 = scores(n_clear, n_diag, 0)
    z1 = scores(n_clear, n_diag, 1)
    zbuf_ref[...] = scores(jnp.maximum(group_start(1), 0), nb, 0)
    w0, rest0 = weights(z0, n_diag, ones, True)
    acc0 = values(n_clear, n_diag, 0, w0)
    w1, rest1 = weights(z1, n_diag, ones, True)
    acc1 = jnp.zeros((SB_HEADDIM, tq), F32)
    if n_diag > nb:
        acc1 = values(n_clear, n_diag, 1, w1)
        wbuf_ref[...] = jnp.zeros(wbuf_ref.shape, BF16)
    else:
        if n_diag < nb:
            wbuf_ref[n_diag * KEY_BLOCK:, :] = jnp.zeros(((nb - n_diag) * KEY_BLOCK, tq), BF16)
        wbuf_ref[0:n_diag * KEY_BLOCK, :] = w1

    def far(t, carry):
        rest0, acc0, rest1, acc1 = carry
        j0 = group_start(t)
        acc1 = acc1 + values(j0 + nb, nb, 1, wbuf_ref[...])
        z1 = scores(j0, nb, 1)
        w0, rest0 = weights(zbuf_ref[...], nb, rest0, False)
        acc0 = acc0 + values(j0, nb, 0, w0)
        w1, rest1 = weights(z1, nb, rest1, False)
        wbuf_ref[...] = w1
        zbuf_ref[...] = scores(jnp.maximum(j0 - nb, 0), nb, 0)
        return rest0, acc0, rest1, acc1

    _, acc0, _, acc1 = lax.fori_loop(1, n_far + 1, far, (rest0, acc0, rest1, acc1), unroll=True)
    acc1 = acc1 + values(group_start(n_far), nb, 1, wbuf_ref[...])
    o_ref[0, qi * tq:(qi + 1) * tq, :] = jnp.concatenate([acc0, acc1], axis=0).T.astype(o_ref.dtype)


def _attn_body(q_ref, k_ref, vt_ref, o_ref, zbuf_ref, wbuf_ref, *, tq, q_pos0):
    for qi in range(q_ref.shape[1] // tq):
        _attn_tile(qi, q_ref, k_ref, vt_ref, o_ref, zbuf_ref, wbuf_ref, tq=tq, q_pos0=q_pos0)


def sb_attention(q, kp, vt, *, tq, q_pos0):
    b, sq, d = q.shape
    pairs, nkb = kp.shape[1], kp.shape[2]
    n_tiles = sq // tq
    for t in range(n_tiles):
        assert (q_pos0 + t * tq) % (KEY_BLOCK * KEY_BLOCKS_PER_STEP) == 0
    assert q_pos0 + sq <= nkb * KEY_BLOCK
    body = functools.partial(_attn_body, tq=tq, q_pos0=q_pos0)
    return pl.pallas_call(
        body,
        grid=(b, pairs),
        in_specs=[pl.BlockSpec((1, sq, LANES), lambda i, p: (i, 0, p)),
                  pl.BlockSpec((1, 1, nkb, KEY_BLOCK, LANES), lambda i, p: (i, p, 0, 0, 0)),
                  pl.BlockSpec((1, 1, nkb, LANES, KEY_BLOCK), lambda i, p: (i, p, 0, 0, 0))],
        out_specs=pl.BlockSpec((1, sq, LANES), lambda i, p: (i, 0, p)),
        out_shape=jax.ShapeDtypeStruct((b, sq, d), BF16),
        scratch_shapes=[pltpu.VMEM((KEY_BLOCKS_PER_STEP * KEY_BLOCK, tq), F32),
                        pltpu.VMEM((KEY_BLOCKS_PER_STEP * KEY_BLOCK, tq), BF16)],
        compiler_params=_params("arbitrary", "arbitrary"),
        name="sb_attention",
    )(q, kp, vt)


def _row_tile(t):
    return 1024 if t % 1024 == 0 else t


def _ssd_layer(y, b, s, conv_state, ssm_state, g_mix, in_w, conv_w, conv_b, dt_bias, a_log, d_skip,
               norm_w, out_w):
    t, d = y.shape
    n_heads = dt_bias.shape[0]
    n_main = in_w.shape[1] - n_heads
    w_dt = jnp.pad(in_w[:, n_main:], ((0, 0), (0, LANES - n_heads))).astype(BF16)
    zx, dt_raw = norm_matmul(y, g_mix, in_w[:, :n_main].astype(BF16), _row_tile(t), 2048,
                             out_dtype=BF16, w_side=w_dt)
    zx = zx.reshape(b, s, n_main)
    dt_raw = dt_raw.reshape(b, s, LANES)
    valid = SSD_CHUNK
    if s % SSD_CHUNK:
        assert s < SSD_CHUNK
        valid = s
        zx = jnp.pad(zx, ((0, 0), (0, SSD_CHUNK - s), (0, 0)))
        dt_raw = jnp.pad(dt_raw, ((0, 0), (0, SSD_CHUNK - s), (0, 0)))
    y_ssd, conv_out, ssm_out = ssd_scan(zx, dt_raw, conv_state, ssm_state, conv_w, conv_b, dt_bias,
                                        a_log, d_skip, norm_w, valid=valid)
    y_ssd = y_ssd[:, :s].reshape(t, -1)
    return matmul_res(y_ssd, out_w.astype(BF16), y, _row_tile(t)), conv_out, ssm_out


def _sb_layer(y, b, s, cache_k, cache_v, slot, n_slots, stacks, g_mix, qkv_w, q_gain, k_gain, out_w):
    t, d = y.shape
    tm = _row_tile(t)
    w = qkv_w.astype(BF16)
    if cache_k is None:
        tq, q_pos0 = KEY_BLOCKS_PER_STEP * KEY_BLOCK, 0
        q, ks, vs, kp, vt = qkv_project(y, g_mix, w, q_gain, k_gain, min(tm, 512), slot=slot,
                                        n_slots=n_slots, stacks=stacks, seq=s)
        q = q.reshape(b, s, d)
    else:
        past = cache_k.shape[1]
        tq, q_pos0 = KEY_BLOCK, past
        q, ks, vs = qkv_project(y, g_mix, w, q_gain, k_gain, min(tm, 512), slot=slot,
                                n_slots=n_slots, stacks=stacks)
        pad = ((0, 0), (0, tq - s), (0, 0))
        kp_c, vt_c = kv_layout(cache_k, cache_v, b, batch0=slot * b)
        kp_n, vt_n = kv_layout(jnp.pad(ks[slot].reshape(b, s, d), pad),
                               jnp.pad(vs[slot].reshape(b, s, d), pad), b)
        kp = jnp.concatenate([kp_c, kp_n], axis=2)
        vt = jnp.concatenate([vt_c, vt_n], axis=2)
        q = jnp.pad(q.reshape(b, s, d), pad)
    o = sb_attention(q, kp, vt, tq=tq, q_pos0=q_pos0)[:, :s].reshape(t, d)
    return matmul_res(o, out_w.astype(BF16), y, tm), (ks, vs)


def kernel(x_prompt, x_sample, state_ssm, state_conv, cache_k, cache_v, norm_mix, norm_mlp, ssd_in_w, ssd_conv_w, ssd_conv_b, ssd_dt_bias, ssd_a_log, ssd_d, ssd_norm_w, ssd_out_w, sb_qkv_w, sb_q_gain, sb_k_gain, sb_out_w, mlp_up, mlp_down):
    bp, sp, d = x_prompt.shape
    bs, ss, _ = x_sample.shape
    depth = norm_mix.shape[0]
    y_p = x_prompt.reshape(bp * sp, d)
    y_s = x_sample.reshape(bs * ss, d)
    n_sb = sb_qkv_w.shape[0]
    heads = d // SB_HEADDIM
    p_ssm, p_conv, s_ssm, s_conv = [], [], [], []
    p_kv = s_kv = None
    past = cache_k.shape[2]
    cache_k2 = cache_k.reshape(n_sb * bs, past, heads, SB_HEADDIM)
    cache_v2 = cache_v.reshape(n_sb * bs, past, heads, SB_HEADDIM)
    for i in range(depth):
        j = i // 2
        if i % 2 == 0:
            prm = (norm_mix[i], ssd_in_w[j], ssd_conv_w[j], ssd_conv_b[j], ssd_dt_bias[j],
                   ssd_a_log[j], ssd_d[j], ssd_norm_w[j], ssd_out_w[j])
            y_p, c_p, st_p = _ssd_layer(y_p, bp, sp, None, None, *prm)
            y_s, c_s, st_s = _ssd_layer(y_s, bs, ss, state_conv[j], state_ssm[j], *prm)
            p_ssm.append(st_p); p_conv.append(c_p)
            s_ssm.append(st_s); s_conv.append(c_s)
        else:
            prm = (norm_mix[i], sb_qkv_w[j], sb_q_gain[j], sb_k_gain[j], sb_out_w[j])
            y_p, p_kv = _sb_layer(y_p, bp, sp, None, None, j, n_sb, p_kv, *prm)
            y_s, s_kv = _sb_layer(y_s, bs, ss, cache_k2, cache_v2, j, n_sb, s_kv, *prm)
        wu = mlp_up[i].astype(BF16)
        wd = mlp_down[i].astype(BF16)
        y_p = mlp(y_p, norm_mlp[i], wu, wd, _row_tile(y_p.shape[0]), 1024)
        y_s = mlp(y_s, norm_mlp[i], wu, wd, _row_tile(y_s.shape[0]), 1024)
    p_k, p_v = (a.reshape(n_sb, bp, sp, heads, SB_HEADDIM) for a in p_kv)
    s_k, s_v = (a.reshape(n_sb, bs, ss, heads, SB_HEADDIM) for a in s_kv)
    return (y_p.reshape(bp, sp, d), y_s.reshape(bs, ss, d),
            jnp.stack(p_ssm), jnp.stack(p_conv), p_k, p_v,
            jnp.stack(s_ssm), jnp.stack(s_conv), s_k, s_v)
```
